```python
import math
import jax, jax.numpy as jnp
from jax import lax
import numpy as np

D_MODEL = 1024
BATCH = 8
SEQ = 4096
DEPTH = 1

W_CONV = D_MODEL
CONV_WIDTH = 3
HEAD_DIM = 64
HEADS_PER_GROUP = 8
GROUP_CONFIGS = ((128, 1), (512, 4), (2048, 16))
N_GROUPS = len(GROUP_CONFIGS)
N_ATTN_HEADS = N_GROUPS * HEADS_PER_GROUP
W_QKV = N_ATTN_HEADS * HEAD_DIM
W_ATTN_OUT = HEADS_PER_GROUP * HEAD_DIM
BLOCK_Q = 64
ROPE_THETA = 10000.0
NORM_EPS = 1e-6

SPLIT_WIDTHS = (W_CONV, W_CONV, W_CONV, W_CONV,
                W_QKV, W_QKV, W_QKV,
                W_ATTN_OUT,
                D_MODEL, D_MODEL)
IN_WIDTH = sum(SPLIT_WIDTHS)
SPLIT_POINTS = tuple(int(v) for v in np.cumsum(SPLIT_WIDTHS)[:-1])

kernel_name = "hybrid_shortconv_dilated_attn_gated_merge"


def rms_norm(x, g):
    x32 = x.astype(jnp.float32)
    y = x32 * lax.rsqrt(jnp.mean(x32 * x32, axis=-1, keepdims=True) + NORM_EPS)
    return (y * g.astype(jnp.float32)).astype(x.dtype)


def rotary(t, positions):
    half = HEAD_DIM // 2
    inv_freq = ROPE_THETA ** (-jnp.arange(0, half, dtype=jnp.float32) / half)
    ang = positions.astype(jnp.float32)[:, None] * inv_freq[None, :]
    cos = jnp.cos(ang)[None, :, None, :]
    sin = jnp.sin(ang)[None, :, None, :]
    t32 = t.astype(jnp.float32)
    t1, t2 = t32[..., :half], t32[..., half:]
    out = jnp.concatenate([t1 * cos - t2 * sin, t2 * cos + t1 * sin], axis=-1)
    return out.astype(t.dtype)


def dilated_band_attention(q, k, v, window, dilation):
    B, S, H, Dh = q.shape
    half = window // (2 * dilation)
    L = S // dilation
    bq = math.gcd(L, BLOCK_Q)
    nb = L // bq
    span = bq + 2 * half

    def to_res(t):
        return t.reshape(B, L, dilation, H, Dh).transpose(0, 2, 1, 3, 4)

    qr, kr, vr = to_res(q), to_res(k), to_res(v)
    pad = ((0, 0), (0, 0), (half, half), (0, 0), (0, 0))
    kp, vp = jnp.pad(kr, pad), jnp.pad(vr, pad)
    idx = jnp.arange(nb)[:, None] * bq + jnp.arange(span)[None, :]
    kb = kp[:, :, idx]
    vb = vp[:, :, idx]
    qb = qr.reshape(B, dilation, nb, bq, H, Dh)

    scale = 1.0 / math.sqrt(Dh)
    scores = jnp.einsum('brnqhd,brnkhd->brnhqk', qb.astype(jnp.float32),
                        kb.astype(jnp.float32)) * scale
    qi = jnp.arange(bq)[:, None]
    kj = jnp.arange(span)[None, :]
    rel = kj - qi
    band = (rel >= 0) & (rel <= 2 * half)
    kpos = idx - half
    valid = (kpos >= 0) & (kpos < L)
    mask = band[None, :, :] & valid[:, None, :]
    scores = jnp.where(mask[None, None, :, None, :, :], scores, -jnp.inf)
    lse = jax.nn.logsumexp(scores, axis=-1)
    p = jnp.exp(scores - lse[..., None])
    o = jnp.einsum('brnhqk,brnkhd->brnqhd', p.astype(v.dtype), vb)

    o = o.reshape(B, dilation, L, H, Dh).transpose(0, 2, 1, 3, 4).reshape(B, S, H, Dh)
    lse = lse.transpose(0, 1, 2, 4, 3).reshape(B, dilation, L, H)
    lse = lse.transpose(0, 2, 1, 3).reshape(B, S, H)
    return o, lse


def setup_inputs(seed: int = 0) -> dict:
    key = jax.random.key(seed)
    ks = jax.random.split(key, 11)
    f32 = jnp.float32
    x = jax.random.normal(ks[0], (BATCH, SEQ, D_MODEL), f32)
    norm_g = 1.0 + 0.02 * jax.random.normal(ks[1], (D_MODEL,), f32)
    w_in = jax.random.normal(ks[2], (D_MODEL, IN_WIDTH), f32) * D_MODEL ** -0.5
    conv_w = jax.random.normal(ks[3], (CONV_WIDTH, W_CONV), f32) * CONV_WIDTH ** -0.5
    conv_b = 0.02 * jax.random.normal(ks[4], (W_CONV,), f32)
    q_norm_g = 1.0 + 0.02 * jax.random.normal(ks[5], (HEAD_DIM,), f32)
    k_norm_g = 1.0 + 0.02 * jax.random.normal(ks[6], (HEAD_DIM,), f32)
    w_branch_conv = jax.random.normal(ks[7], (W_CONV, D_MODEL), f32) * W_CONV ** -0.5
    w_branch_attn = jax.random.normal(ks[8], (W_ATTN_OUT, D_MODEL), f32) * W_ATTN_OUT ** -0.5
    w_out = jax.random.normal(ks[9], (D_MODEL, D_MODEL), f32) * D_MODEL ** -0.5
    return {"x": x, "norm_g": norm_g, "w_in": w_in, "conv_w": conv_w, "conv_b": conv_b,
            "q_norm_g": q_norm_g, "k_norm_g": k_norm_g, "w_branch_conv": w_branch_conv,
            "w_branch_attn": w_branch_attn, "w_out": w_out}


def reference(x, norm_g, w_in, conv_w, conv_b, q_norm_g, k_norm_g,
              w_branch_conv, w_branch_attn, w_out):
    B, S, _ = x.shape
    positions = jnp.arange(S)
    for _layer in range(DEPTH):
        xn = rms_norm(x, norm_g)
        proj = jnp.einsum('bsd,de->bse', xn, w_in)
        (b_c, c_c, h_c, z_c, q, k, v, z_a, g_c, g_a) = jnp.split(proj, SPLIT_POINTS, axis=-1)

        u = c_c * h_c
        up = jnp.pad(u, ((0, 0), (1, 1), (0, 0)))
        conv = up[:, :-2] * conv_w[0] + up[:, 1:-1] * conv_w[1] + up[:, 2:] * conv_w[2] + conv_b
        y_c = b_c * conv * jax.nn.silu(z_c)

        q = q.reshape(B, S, N_ATTN_HEADS, HEAD_DIM)
        k = k.reshape(B, S, N_ATTN_HEADS, HEAD_DIM)
        v = v.reshape(B, S, N_ATTN_HEADS, HEAD_DIM)
        q = rotary(rms_norm(q, q_norm_g), positions)
        k = rotary(rms_norm(k, k_norm_g), positions)
        outs, lses = [], []
        for g, (window, dilation) in enumerate(GROUP_CONFIGS):
            hs = slice(g * HEADS_PER_GROUP, (g + 1) * HEADS_PER_GROUP)
            o_g, lse_g = dilated_band_attention(q[:, :, hs], k[:, :, hs], v[:, :, hs],
                                                window, dilation)
            outs.append(o_g)
            lses.append(lse_g)
        o_all = jnp.stack(outs, axis=0)
        w_den = jax.nn.softmax(jnp.stack(lses, axis=0), axis=0)
        o_comb = jnp.sum(w_den[..., None].astype(o_all.dtype) * o_all, axis=0)
        y_a = o_comb.reshape(B, S, W_ATTN_OUT) * jax.nn.silu(z_a)

        m = (jax.nn.sigmoid(g_c) * jnp.einsum('bsc,cd->bsd', y_c, w_branch_conv)
             + jax.nn.sigmoid(g_a) * jnp.einsum('bsc,cd->bsd', y_a, w_branch_attn))
        x = x + jnp.einsum('bsd,de->bse', m, w_out)
    return x
```

```python
import functools
import math

import jax
import jax.numpy as jnp
import numpy as np
from jax import lax
from jax.experimental import pallas as pl
from jax.experimental.pallas import tpu as pltpu

D_MODEL = 1024
W_CONV = 1024
HEAD_DIM = 64
HEADS_PER_GROUP = 8
DILATIONS = (1, 4, 16)
BAND_HALF = 64
N_GROUPS = 3
W_QKV = N_GROUPS * HEADS_PER_GROUP * HEAD_DIM
W_ATTN_OUT = HEADS_PER_GROUP * HEAD_DIM
ROPE_THETA = 10000.0
NORM_EPS = 1e-6

OFF_B, OFF_C, OFF_H, OFF_ZC = 0, 1024, 2048, 3072
OFF_Q = 4096
OFF_K = OFF_Q + W_QKV
OFF_V = OFF_K + W_QKV
OFF_ZA = OFF_V + W_QKV
OFF_GC = OFF_ZA + W_ATTN_OUT
OFF_GA = OFF_GC + D_MODEL
IN_WIDTH = OFF_GA + D_MODEL

LANES = 128
MXU_WIDTH = 256
HP_WIDTH = 2 * HEAD_DIM
N_HP_GROUP = HEADS_PER_GROUP // 2
N_HP = N_GROUPS * N_HP_GROUP
BF16_ROWS = 16

TM_IN = 512
TM_OUT = 512
CHUNK = MXU_WIDTH
BQ = 128
SPAN = BQ + 2 * BAND_HALF
NEG_BIG = -1e30

VMEM_LIMIT_IN = 56 * 1024 * 1024
VMEM_LIMIT_ATTN = 48 * 1024 * 1024
VMEM_LIMIT_OUT = 48 * 1024 * 1024

f32 = jnp.float32
bf16 = jnp.bfloat16


def _sigmoid(z):
    return 1.0 / (1.0 + jnp.exp(-z))


def _in_proj_body(x_ref, ng_ref, w_ref, cos_ref, sin_ref, qg_ref, kg_ref, ones_ref,
                  u_ref, a_ref, q_ref, k_ref, v_ref, sza_ref, sgc_ref, sga_ref, xn_ref):
    x = x_ref[...]
    ms = jnp.mean(x * x, axis=-1, keepdims=True)
    xn_ref[...] = (x * lax.rsqrt(ms + NORM_EPS) * ng_ref[...]).astype(bf16)

    def proj(c0):
        return jnp.dot(xn_ref[...], w_ref[:, c0:c0 + CHUNK], preferred_element_type=f32)

    for j in range(0, W_CONV, CHUNK):
        u_ref[:, j:j + CHUNK] = (proj(OFF_C + j) * proj(OFF_H + j)).astype(bf16)
        z = proj(OFF_ZC + j)
        a_ref[:, j:j + CHUNK] = (proj(OFF_B + j) * z * _sigmoid(z)).astype(bf16)

    lane = lax.broadcasted_iota(jnp.int32, (1, HP_WIDTH), 1)
    first_half = (lane % HEAD_DIM) < (HEAD_DIM // 2)
    cos = cos_ref[...]
    sin = sin_ref[...]
    for off, g_ref, o_ref, scale in ((OFF_Q, qg_ref, q_ref, 1.0 / math.sqrt(HEAD_DIM)),
                                     (OFF_K, kg_ref, k_ref, 1.0)):
        for c in range(W_QKV // CHUNK):
            t = proj(off + c * CHUNK)
            ss = jnp.dot((t * t).astype(bf16), ones_ref[...], preferred_element_type=f32)
            r = lax.rsqrt(ss * (1.0 / HEAD_DIM) + NORM_EPS) * scale
            for p in range(CHUNK // HP_WIDTH):
                cols = slice(p * HP_WIDTH, (p + 1) * HP_WIDTH)
                tg = t[:, cols] * g_ref[...]
                partner = jnp.where(first_half,
                                    pltpu.roll(tg, HP_WIDTH - HEAD_DIM // 2, axis=1),
                                    pltpu.roll(tg, HEAD_DIM // 2, axis=1))
                out = (tg * cos + partner * sin) * r[:, cols]
                o_ref[c * (CHUNK // HP_WIDTH) + p] = out.astype(bf16)

    for c in range(W_QKV // CHUNK):
        t = proj(OFF_V + c * CHUNK).astype(bf16)
        for p in range(CHUNK // HP_WIDTH):
            v_ref[c * (CHUNK // HP_WIDTH) + p] = t[:, p * HP_WIDTH:(p + 1) * HP_WIDTH]

    for j in range(0, W_ATTN_OUT, CHUNK):
        z = proj(OFF_ZA + j)
        sza_ref[:, j:j + CHUNK] = (z * _sigmoid(z)).astype(bf16)
    for j in range(0, D_MODEL, CHUNK):
        sgc_ref[:, j:j + CHUNK] = _sigmoid(proj(OFF_GC + j)).astype(bf16)
        sga_ref[:, j:j + CHUNK] = _sigmoid(proj(OFF_GA + j)).astype(bf16)


def _in_proj(x, norm_g, w_in, cos_t, sin_t, qg, kg, ones_bd):
    B, S, _ = x.shape
    nt = S // TM_IN
    const = lambda b, t: (0, 0)
    tok = lambda b, t: (b, t, 0)
    hp_major = lambda b, t: (b, 0, t, 0)
    sq = pl.Squeezed()
    out_shape = (
        jax.ShapeDtypeStruct((B, S, W_CONV), bf16),
        jax.ShapeDtypeStruct((B, S, W_CONV), bf16),
        jax.ShapeDtypeStruct((B, N_HP, S, HP_WIDTH), bf16),
        jax.ShapeDtypeStruct((B, N_HP, S, HP_WIDTH), bf16),
        jax.ShapeDtypeStruct((B, N_HP, S, HP_WIDTH), bf16),
        jax.ShapeDtypeStruct((B, S, W_ATTN_OUT), bf16),
        jax.ShapeDtypeStruct((B, S, D_MODEL), bf16),
        jax.ShapeDtypeStruct((B, S, D_MODEL), bf16),
    )
    tok_spec = lambda w: pl.BlockSpec((sq, TM_IN, w), tok)
    hp_spec = pl.BlockSpec((sq, N_HP, TM_IN, HP_WIDTH), hp_major)
    return pl.pallas_call(
        _in_proj_body,
        grid=(B, nt),
        in_specs=[
            tok_spec(D_MODEL),
            pl.BlockSpec((1, D_MODEL), const),
            pl.BlockSpec((D_MODEL, IN_WIDTH), const, pipeline_mode=pl.Buffered(1)),
            pl.BlockSpec((TM_IN, HP_WIDTH), lambda b, t: (t, 0)),
            pl.BlockSpec((TM_IN, HP_WIDTH), lambda b, t: (t, 0)),
            pl.BlockSpec((1, HP_WIDTH), const),
            pl.BlockSpec((1, HP_WIDTH), const),
            pl.BlockSpec((CHUNK, CHUNK), const),
        ],
        out_specs=(tok_spec(W_CONV), tok_spec(W_CONV), hp_spec, hp_spec, hp_spec,
                   tok_spec(W_ATTN_OUT), tok_spec(D_MODEL), tok_spec(D_MODEL)),
        out_shape=out_shape,
        scratch_shapes=[pltpu.VMEM((TM_IN, D_MODEL), bf16)],
        compiler_params=pltpu.CompilerParams(
            dimension_semantics=("arbitrary", "arbitrary"),
            vmem_limit_bytes=VMEM_LIMIT_IN),
        name="in_proj",
    )(x, norm_g, w_in, cos_t, sin_t, qg, kg, ones_bd)


def _attn_block(q, ks, vs, off):
    lane = lax.broadcasted_iota(jnp.int32, (BQ, HP_WIDTH), 1)
    head0 = lane < HEAD_DIM
    zero = jnp.zeros_like(q)
    q2 = jnp.concatenate([jnp.where(head0, q, zero), jnp.where(head0, zero, q)], axis=0)
    s = lax.dot_general(q2, ks, (((1,), (1,)), ((), ())), preferred_element_type=f32)
    qi = lax.broadcasted_iota(jnp.int32, (2 * BQ, SPAN), 0) % BQ
    kj = lax.broadcasted_iota(jnp.int32, (2 * BQ, SPAN), 1)
    rel = kj - qi + off
    s = jnp.where(jnp.abs(rel) <= BAND_HALF, s, NEG_BIG)
    m = jnp.max(s, axis=1, keepdims=True)
    p = jnp.exp(s - m)
    l = jnp.sum(p, axis=1, keepdims=True)
    pv = jnp.dot(p.astype(bf16), vs, preferred_element_type=f32)
    o2 = pv * (1.0 / l)
    lse2 = m + jnp.log(l)
    o = jnp.where(head0, o2[:BQ], o2[BQ:])
    lse = jnp.where(head0, lse2[:BQ], lse2[BQ:])
    return o, lse


def _band_attn_body(q0_ref, k0_ref, v0_ref, q1_ref, k1_ref, v1_ref, q2_ref, k2_ref, v2_ref,
                    o_ref, o0_s, l0_s, o1_s, l1_s):
    d1, d2 = DILATIONS[1], DILATIONS[2]
    L0 = q0_ref.shape[0]
    L1 = q1_ref.shape[0]
    L2 = q2_ref.shape[0]

    def slab(n, L):
        q_start = pl.multiple_of(n * BQ, BQ)
        start = pl.multiple_of(jnp.clip(n * BQ - BAND_HALF, 0, L - SPAN), BAND_HALF)
        return q_start, start, start - n * BQ

    def g0_step(n, carry):
        qs, st, off = slab(n, L0)
        o, lse = _attn_block(q0_ref[pl.ds(qs, BQ), :], k0_ref[pl.ds(st, SPAN), :],
                             v0_ref[pl.ds(st, SPAN), :], off)
        o0_s[pl.ds(qs, BQ), :] = o
        l0_s[pl.ds(qs, BQ), :] = lse
        return carry

    lax.fori_loop(0, L0 // BQ, g0_step, 0)

    for r in range(d1):
        cols = slice(r * HP_WIDTH, (r + 1) * HP_WIDTH)

        def g1_step(n, carry, r=r, cols=cols):
            qs, st, off = slab(n, L1)
            o, lse = _attn_block(q1_ref[pl.ds(qs, BQ), cols], k1_ref[pl.ds(st, SPAN), cols],
                                 v1_ref[pl.ds(st, SPAN), cols], off)
            o1_s[r, pl.ds(qs, BQ), :] = o
            l1_s[r, pl.ds(qs, BQ), :] = lse
            return carry

        lax.fori_loop(0, L1 // BQ, g1_step, 0)

    for r in range(d2):
        cols = slice(r * HP_WIDTH, (r + 1) * HP_WIDTH)
        r1, m1 = r % d1, r // d1
        for n in range(L2 // BQ):
            st = min(max(n * BQ - BAND_HALF, 0), L2 - SPAN)
            o2, lse2 = _attn_block(q2_ref[n * BQ:(n + 1) * BQ, cols], k2_ref[st:st + SPAN, cols],
                                   v2_ref[st:st + SPAN, cols], st - n * BQ)
            rows0 = pl.ds(n * BQ * d2 + r, BQ, stride=d2)
            rows1 = pl.ds(n * BQ * (d2 // d1) + m1, BQ, stride=d2 // d1)
            o0, lse0 = o0_s[rows0, :], l0_s[rows0, :]
            o1, lse1 = o1_s.at[r1][rows1, :], l1_s.at[r1][rows1, :]
            mx = jnp.maximum(jnp.maximum(lse0, lse1), lse2)
            w0, w1, w2 = jnp.exp(lse0 - mx), jnp.exp(lse1 - mx), jnp.exp(lse2 - mx)
            o = (w0 * o0 + w1 * o1 + w2 * o2) * (1.0 / (w0 + w1 + w2))
            o_ref[n * BQ:(n + 1) * BQ, cols] = o.astype(bf16)


def _band_attn(q, k, v):
    B, _, S, _ = q.shape
    sq = pl.Squeezed()
    views, specs = [], []
    for g, d in enumerate(DILATIONS):
        shape = (B, N_HP, S // d, d * HP_WIDTH)
        spec = pl.BlockSpec((sq, sq, S // d, d * HP_WIDTH),
                            functools.partial(lambda b, h, g: (b, g * N_HP_GROUP + h, 0, 0), g=g))
        for arr in (q, k, v):
            views.append(arr.reshape(shape))
            specs.append(spec)
    d1, d2 = DILATIONS[1], DILATIONS[2]
    out = pl.pallas_call(
        _band_attn_body,
        grid=(B, N_HP_GROUP),
        in_specs=specs,
        out_specs=pl.BlockSpec((sq, sq, S // d2, d2 * HP_WIDTH), lambda b, h: (b, h, 0, 0)),
        out_shape=jax.ShapeDtypeStruct((B, N_HP_GROUP, S // d2, d2 * HP_WIDTH), bf16),
        scratch_shapes=[pltpu.VMEM((S, HP_WIDTH), f32), pltpu.VMEM((S, HP_WIDTH), f32),
                        pltpu.VMEM((d1, S // d1, HP_WIDTH), f32),
                        pltpu.VMEM((d1, S // d1, HP_WIDTH), f32)],
        compiler_params=pltpu.CompilerParams(
            dimension_semantics=("arbitrary", "arbitrary"),
            vmem_limit_bytes=VMEM_LIMIT_ATTN),
        name="band_attn",
    )(*views)
    return out.reshape(B, N_HP_GROUP, S, HP_WIDTH)


def _out_proj_body(x_ref, u_ref, up_ref, un_ref, a_ref, o_ref, sza_ref, sgc_ref, sga_ref,
                   cw_ref, cb_ref, wbc_ref, wba_ref, wo_ref, out_ref):
    t = pl.program_id(1)
    nt = pl.num_programs(1)
    u = u_ref[...].astype(f32)
    row = lax.broadcasted_iota(jnp.int32, (TM_OUT, 1), 0)
    prev_row = up_ref[...].astype(f32)[BF16_ROWS - 1:BF16_ROWS] * jnp.where(t > 0, 1.0, 0.0)
    next_row = un_ref[...].astype(f32)[0:1] * jnp.where(t < nt - 1, 1.0, 0.0)
    u_prev = jnp.where(row == 0, prev_row, pltpu.roll(u, 1, axis=0))
    u_next = jnp.where(row == TM_OUT - 1, next_row, pltpu.roll(u, TM_OUT - 1, axis=0))
    conv = u_prev * cw_ref[0:1, :] + u * cw_ref[1:2, :] + u_next * cw_ref[2:3, :] + cb_ref[...]
    y_c = (a_ref[...].astype(f32) * conv).astype(bf16)
    p_c = jnp.dot(y_c, wbc_ref[...], preferred_element_type=f32)
    o_cat = jnp.concatenate([o_ref[h] for h in range(N_HP_GROUP)], axis=1)
    y_a = o_cat * sza_ref[...]
    p_a = jnp.dot(y_a, wba_ref[...], preferred_element_type=f32)
    m = (sgc_ref[...].astype(f32) * p_c + sga_ref[...].astype(f32) * p_a).astype(bf16)
    out_ref[...] = x_ref[...] + jnp.dot(m, wo_ref[...], preferred_element_type=f32)


def _out_proj(x, u, a, o, sza, sgc, sga, conv_w, conv_b, wbc, wba, wo):
    B, S, _ = x.shape
    nt = S // TM_OUT
    rb = TM_OUT // BF16_ROWS
    sq = pl.Squeezed()
    const = lambda b, t: (0, 0)
    tok = lambda b, t: (b, t, 0)
    tok_spec = lambda w: pl.BlockSpec((sq, TM_OUT, w), tok)
    halo_prev = pl.BlockSpec((sq, BF16_ROWS, W_CONV), lambda b, t: (b, jnp.maximum(t * rb - 1, 0), 0))
    halo_next = pl.BlockSpec((sq, BF16_ROWS, W_CONV),
                             lambda b, t: (b, jnp.minimum((t + 1) * rb, S // BF16_ROWS - 1), 0))
    return pl.pallas_call(
        _out_proj_body,
        grid=(B, nt),
        in_specs=[
            tok_spec(D_MODEL), tok_spec(W_CONV), halo_prev, halo_next, tok_spec(W_CONV),
            pl.BlockSpec((sq, N_HP_GROUP, TM_OUT, HP_WIDTH), lambda b, t: (b, 0, t, 0)),
            tok_spec(W_ATTN_OUT), tok_spec(D_MODEL), tok_spec(D_MODEL),
            pl.BlockSpec((3, W_CONV), const), pl.BlockSpec((1, W_CONV), const),
            pl.BlockSpec((W_CONV, D_MODEL), const), pl.BlockSpec((W_ATTN_OUT, D_MODEL), const),
            pl.BlockSpec((D_MODEL, D_MODEL), const),
        ],
        out_specs=tok_spec(D_MODEL),
        out_shape=jax.ShapeDtypeStruct((B, S, D_MODEL), f32),
        compiler_params=pltpu.CompilerParams(
            dimension_semantics=("arbitrary", "arbitrary"),
            vmem_limit_bytes=VMEM_LIMIT_OUT),
        name="out_proj",
    )(x, u, u, u, a, o, sza, sgc, sga, conv_w, conv_b, wbc, wba, wo)


def _rotary_tables(S):
    half = HEAD_DIM // 2
    inv_freq = ROPE_THETA ** (-jnp.arange(0, half, dtype=f32) / half)
    ang = jnp.arange(S, dtype=f32)[:, None] * inv_freq[None, :]
    cos, sin = jnp.cos(ang), jnp.sin(ang)
    reps = HP_WIDTH // HEAD_DIM
    cos_t = jnp.tile(jnp.concatenate([cos, cos], axis=1), (1, reps))
    sin_t = jnp.tile(jnp.concatenate([-sin, sin], axis=1), (1, reps))
    return cos_t, sin_t


def kernel(x, norm_g, w_in, conv_w, conv_b, q_norm_g, k_norm_g, w_branch_conv, w_branch_attn, w_out):
    B, S, D = x.shape
    assert D == D_MODEL and w_in.shape == (D_MODEL, IN_WIDTH)
    assert S % TM_IN == 0 and S % TM_OUT == 0 and (S // DILATIONS[-1]) % BQ == 0
    cos_t, sin_t = _rotary_tables(S)
    reps = HP_WIDTH // HEAD_DIM
    qg =jnp.tile(q_norm_g.astype(f32), reps)[None, :]
    kg = jnp.tile(k_norm_g.astype(f32), reps)[None, :]
    head_id = np.arange(CHUNK) // HEAD_DIM
    ones_bd = jnp.asarray(head_id[:, None] == head_id[None, :], dtype=bf16)

    u, a, q, k, v, sza, sgc, sga = _in_proj(
        x, norm_g.astype(f32)[None, :], w_in.astype(bf16), cos_t, sin_t, qg, kg, ones_bd)
    o = _band_attn(q, k, v)
    return _out_proj(x, u, a, o, sza, sgc, sga, conv_w.astype(f32), conv_b.astype(f32)[None, :],
                     w_branch_conv.astype(bf16), w_branch_attn.astype(bf16), w_out.astype(bf16))
```

```python
import math

import jax
import jax.numpy as jnp
import numpy as np
from jax import lax
from jax.experimental import pallas as pl
from jax.experimental.pallas import tpu as pltpu

D_MODEL = 1024
W_CONV = 1024
HEAD_DIM = 64
HEADS_PER_GROUP = 8
DILATIONS = (1, 4, 16)
BAND_HALF = 64
N_GROUPS = 3
W_QKV = N_GROUPS * HEADS_PER_GROUP * HEAD_DIM
W_ATTN_OUT = HEADS_PER_GROUP * HEAD_DIM
ROPE_THETA = 10000.0
NORM_EPS = 1e-6

OFF_B, OFF_C, OFF_H, OFF_ZC = 0, 1024, 2048, 3072
OFF_Q = 4096
OFF_K = OFF_Q + W_QKV
OFF_V = OFF_K + W_QKV
OFF_ZA = OFF_V + W_QKV
OFF_GC = OFF_ZA + W_ATTN_OUT
OFF_GA = OFF_GC + D_MODEL
IN_WIDTH = OFF_GA + D_MODEL

LANES = 128
MXU_WIDTH = 256
HP_WIDTH = 2 * HEAD_DIM
N_HP_GROUP = HEADS_PER_GROUP // 2
BF16_ROWS = 16

TM_IN = 512
TM_OUT = 512
CHUNK = MXU_WIDTH
HP_PER_CHUNK = CHUNK // HP_WIDTH
N_PERM = 4
BQ = 128
SPAN = BQ + 2 * BAND_HALF
N_BIAS = SPAN // BAND_HALF - 1
NEG_BIG = -1e30
LOG2E = math.log2(math.e)

VMEM_LIMIT_IN = 58 * 1024 * 1024
VMEM_LIMIT_ATTN = 48 * 1024 * 1024
VMEM_LIMIT_OUT = 48 * 1024 * 1024

f32 = jnp.float32
bf16 = jnp.bfloat16


def _sigmoid(z):
    return 1.0 / (1.0 + jnp.exp(-z))


def _in_proj_body(x_ref, ng_ref, w_ref, cos_ref, sin_ref, qg_ref, kg_ref, ones_ref,
                  u_ref, a_ref, q0_ref, q1_ref, q2_ref, k0_ref, k1_ref, k2_ref,
                  v0_ref, v1_ref, v2_ref, sza_ref, sgc_ref, sga_ref, xn_ref, perm_ref):
    x = x_ref[...]
    ms = jnp.mean(x * x, axis=-1, keepdims=True)
    xn_ref[...] = (x * lax.rsqrt(ms + NORM_EPS) * ng_ref[...]).astype(bf16)

    def proj(c0):
        return jnp.dot(xn_ref[...], w_ref[:, c0:c0 + CHUNK], preferred_element_type=f32)

    n_perm_used = [0]

    def store_head_pair(group_refs, hp, val):
        g, h = divmod(hp, N_HP_GROUP)
        d = DILATIONS[g]
        if d == 1:
            group_refs[g][h] = val.astype(bf16)
            return
        buf = perm_ref.at[n_perm_used[0] % N_PERM]
        n_perm_used[0] += 1
        buf[...] = val
        for r in range(d):
            group_refs[g][h, :, r * HP_WIDTH:(r + 1) * HP_WIDTH] = (
                buf[pl.ds(r, TM_IN // d, stride=d), :].astype(bf16))

    for j in range(0, W_CONV, CHUNK):
        u_ref[:, j:j + CHUNK] = (proj(OFF_C + j) * proj(OFF_H + j)).astype(bf16)
        z = proj(OFF_ZC + j)
        a_ref[:, j:j + CHUNK] = (proj(OFF_B + j) * z * _sigmoid(z)).astype(bf16)

    lane = lax.broadcasted_iota(jnp.int32, (1, HP_WIDTH), 1)
    first_half = (lane % HEAD_DIM) < (HEAD_DIM // 2)
    cos = cos_ref[...]
    sin = sin_ref[...]
    for off, g_ref, refs, scale in ((OFF_Q, qg_ref, (q0_ref, q1_ref, q2_ref), LOG2E / math.sqrt(HEAD_DIM)),
                                    (OFF_K, kg_ref, (k0_ref, k1_ref, k2_ref), 1.0)):
        for c in range(W_QKV // CHUNK):
            t = proj(off + c * CHUNK)
            ss = jnp.dot((t * t).astype(bf16), ones_ref[...], preferred_element_type=f32)
            r = lax.rsqrt(ss * (1.0 / HEAD_DIM) + NORM_EPS) * scale
            for p in range(HP_PER_CHUNK):
                cols = slice(p * HP_WIDTH, (p + 1) * HP_WIDTH)
                tg = t[:, cols] * g_ref[...]
                partner = jnp.where(first_half,
                                    pltpu.roll(tg, HP_WIDTH - HEAD_DIM // 2, axis=1),
                                    pltpu.roll(tg, HEAD_DIM // 2, axis=1))
                store_head_pair(refs, c * HP_PER_CHUNK + p, (tg * cos + partner * sin) * r[:, cols])

    for c in range(W_QKV // CHUNK):
        t = proj(OFF_V + c * CHUNK)
        for p in range(HP_PER_CHUNK):
            store_head_pair((v0_ref, v1_ref, v2_ref), c * HP_PER_CHUNK + p,
                            t[:, p * HP_WIDTH:(p + 1) * HP_WIDTH])

    for j in range(0, W_ATTN_OUT, CHUNK):
        z = proj(OFF_ZA + j)
        sza_ref[:, j:j + CHUNK] = (z * _sigmoid(z)).astype(bf16)
    for j in range(0, D_MODEL, CHUNK):
        sgc_ref[:, j:j + CHUNK] = _sigmoid(proj(OFF_GC + j)).astype(bf16)
        sga_ref[:, j:j + CHUNK] = _sigmoid(proj(OFF_GA + j)).astype(bf16)


def _in_proj(x, norm_g, w_in, cos_t, sin_t, qg, kg, ones_bd):
    B, S, _ = x.shape
    nt = S // TM_IN
    const = lambda b, t: (0, 0)
    tok = lambda b, t: (b, t, 0)
    sq = pl.Squeezed()
    tok_spec = lambda w: pl.BlockSpec((sq, TM_IN, w), tok)
    tok_shape = lambda w: jax.ShapeDtypeStruct((B, S, w), bf16)
    grp_shapes = tuple(jax.ShapeDtypeStruct((B, N_HP_GROUP, S // d, d * HP_WIDTH), bf16)
                       for d in DILATIONS)
    grp_specs = tuple(pl.BlockSpec((sq, N_HP_GROUP, TM_IN // d, d * HP_WIDTH),
                                   lambda b, t: (b, 0, t, 0)) for d in DILATIONS)
    return pl.pallas_call(
        _in_proj_body,
        grid=(B, nt),
        in_specs=[
            tok_spec(D_MODEL),
            pl.BlockSpec((1, D_MODEL), const),
            pl.BlockSpec((D_MODEL, IN_WIDTH), const, pipeline_mode=pl.Buffered(1)),
            pl.BlockSpec((TM_IN, HP_WIDTH), lambda b, t: (t, 0)),
            pl.BlockSpec((TM_IN, HP_WIDTH), lambda b, t: (t, 0)),
            pl.BlockSpec((1, HP_WIDTH), const),
            pl.BlockSpec((1, HP_WIDTH), const),
            pl.BlockSpec((CHUNK, CHUNK), const),
        ],
        out_specs=(tok_spec(W_CONV), tok_spec(W_CONV)) + grp_specs * 3
                  + (tok_spec(W_ATTN_OUT), tok_spec(D_MODEL), tok_spec(D_MODEL)),
        out_shape=(tok_shape(W_CONV), tok_shape(W_CONV)) + grp_shapes * 3
                  + (tok_shape(W_ATTN_OUT), tok_shape(D_MODEL), tok_shape(D_MODEL)),
        scratch_shapes=[pltpu.VMEM((TM_IN, D_MODEL), bf16),
                        pltpu.VMEM((N_PERM, TM_IN, HP_WIDTH), f32)],
        compiler_params=pltpu.CompilerParams(
            dimension_semantics=("arbitrary", "arbitrary"),
            vmem_limit_bytes=VMEM_LIMIT_IN),
        name="in_proj",
    )(x, norm_g, w_in, cos_t, sin_t, qg, kg, ones_bd)


def _attn_block(q, ks, vs, bias):
    lane = lax.broadcasted_iota(jnp.int32, (BQ, HP_WIDTH), 1)
    head0 = lane < HEAD_DIM
    zero = jnp.zeros_like(q)
    q2 = jnp.concatenate([jnp.where(head0, q, zero), jnp.where(head0, zero, q)], axis=0)
    s = lax.dot_general(q2, ks, (((1,), (1,)), ((), ())), preferred_element_type=f32) + bias
    m = jnp.max(s, axis=1, keepdims=True)
    p = jnp.exp2(s - m)
    l = jnp.sum(p, axis=1, keepdims=True)
    pv = jnp.dot(p.astype(bf16), vs, preferred_element_type=f32)
    o2 = pv * (1.0 / l)
    lse2 = m + jnp.log2(l)
    o = jnp.where(head0, o2[:BQ], o2[BQ:])
    lse = jnp.where(head0, lse2[:BQ], lse2[BQ:])
    return o, lse


def _band_attn_body(bias_ref, q0_ref, k0_ref, v0_ref, q1_ref, k1_ref, v1_ref,
                    q2_ref, k2_ref, v2_ref, o_ref, o0_s, l0_s, o1_s, l1_s):
    d1, d2 = DILATIONS[1], DILATIONS[2]
    L0, L1, L2 = q0_ref.shape[0], q1_ref.shape[0], q2_ref.shape[0]

    def block(q_ref, k_ref, v_ref, cols, n, L):
        if isinstance(n, int):
            qs = n * BQ
            st = min(max(qs - BAND_HALF, 0), L - SPAN)
            case = (qs - st) // BAND_HALF
        else:
            qs = pl.multiple_of(n * BQ, BQ)
            st = pl.multiple_of(jnp.clip(n * BQ - BAND_HALF, 0, L - SPAN), BAND_HALF)
            case = lax.shift_right_logical(qs - st, BAND_HALF.bit_length() - 1)
        o, lse = _attn_block(q_ref[pl.ds(qs, BQ), cols], k_ref[pl.ds(st, SPAN), cols],
                             v_ref[pl.ds(st, SPAN), cols], bias_ref[case])
        return qs, o, lse

    def g0_step(i, carry):
        for j in range(2):
            qs, o, lse = block(q0_ref, k0_ref, v0_ref, slice(None), 2 * i + j, L0)
            o0_s[pl.ds(qs, BQ), :] = o
            l0_s[pl.ds(qs, BQ), :] = lse
        return carry

    lax.fori_loop(0, L0 // BQ // 2, g0_step, 0)

    def g1_step(n, carry):
        for r in range(d1):
            qs, o, lse = block(q1_ref, k1_ref, v1_ref, slice(r * HP_WIDTH, (r + 1) * HP_WIDTH), n, L1)
            o1_s[r, pl.ds(qs, BQ), :] = o
            l1_s[r, pl.ds(qs, BQ), :] = lse
        return carry

    lax.fori_loop(0, L1 // BQ, g1_step, 0)

    for r in range(d2):
        r1, m1 = r % d1, r // d1
        for n in range(L2 // BQ):
            _, o2, lse2 = block(q2_ref, k2_ref, v2_ref, slice(r * HP_WIDTH, (r + 1) * HP_WIDTH), n, L2)
            rows0 = pl.ds(n * BQ * d2 + r, BQ, stride=d2)
            rows1 = pl.ds(n * BQ * (d2 // d1) + m1, BQ, stride=d2 // d1)
            o0, lse0 = o0_s[rows0, :], l0_s[rows0, :]
            o1, lse1 = o1_s.at[r1][rows1, :], l1_s.at[r1][rows1, :]
            mx = jnp.maximum(jnp.maximum(lse0, lse1), lse2)
            w0, w1, w2 = jnp.exp2(lse0 - mx), jnp.exp2(lse1 - mx), jnp.exp2(lse2 - mx)
            o0_s[rows0, :] = (w0 * o0 + w1 * o1 + w2 * o2) * (1.0 / (w0 + w1 + w2))

    o_ref[...] = o0_s[...].astype(bf16)


def _band_bias():
    qi = np.arange(2 * BQ)[:, None] % BQ
    kj = np.arange(SPAN)[None, :]
    rel = np.stack([kj - qi - c * BAND_HALF for c in range(N_BIAS)])
    return jnp.asarray(np.where(np.abs(rel) <= BAND_HALF, 0.0, NEG_BIG), dtype=f32)


def _band_attn(qkv):
    B = qkv[0].shape[0]
    S = qkv[0].shape[2]
    sq = pl.Squeezed()
    d1 = DILATIONS[1]
    hp_block = lambda b, h: (b, h, 0, 0)
    operands, specs = [], []
    for g, d in enumerate(DILATIONS):
        for a in range(3):
            operands.append(qkv[a * N_GROUPS + g])
            specs.append(pl.BlockSpec((sq, sq, S // d, d * HP_WIDTH), hp_block))
    return pl.pallas_call(
        _band_attn_body,
        grid=(B, N_HP_GROUP),
        in_specs=[pl.BlockSpec((N_BIAS, 2 * BQ, SPAN), lambda b, h: (0, 0, 0))] + specs,
        out_specs=pl.BlockSpec((sq, sq, S, HP_WIDTH), hp_block),
        out_shape=jax.ShapeDtypeStruct((B, N_HP_GROUP, S, HP_WIDTH), bf16),
        scratch_shapes=[pltpu.VMEM((S, HP_WIDTH), f32), pltpu.VMEM((S, HP_WIDTH), f32),
                        pltpu.VMEM((d1, S // d1, HP_WIDTH), f32),
                        pltpu.VMEM((d1, S // d1, HP_WIDTH), f32)],
        compiler_params=pltpu.CompilerParams(
            dimension_semantics=("arbitrary", "arbitrary"),
            vmem_limit_bytes=VMEM_LIMIT_ATTN),
        name="band_attn",
    )(_band_bias(), *operands)


def _out_proj_body(x_ref, u_ref, up_ref, un_ref, a_ref, o_ref, sza_ref, sgc_ref, sga_ref,
                   cw_ref, cb_ref, wbc_ref, wba_ref, wo_ref, out_ref):
    t = pl.program_id(1)
    nt = pl.num_programs(1)
    u = u_ref[...].astype(f32)
    row = lax.broadcasted_iota(jnp.int32, (TM_OUT, 1), 0)
    prev_row = up_ref[...].astype(f32)[BF16_ROWS - 1:BF16_ROWS] * jnp.where(t > 0, 1.0, 0.0)
    next_row = un_ref[...].astype(f32)[0:1] * jnp.where(t < nt - 1, 1.0, 0.0)
    u_prev = jnp.where(row == 0, prev_row, pltpu.roll(u, 1, axis=0))
    u_next = jnp.where(row == TM_OUT - 1, next_row, pltpu.roll(u, TM_OUT - 1, axis=0))
    conv = u_prev * cw_ref[0:1, :] + u * cw_ref[1:2, :] + u_next * cw_ref[2:3, :] + cb_ref[...]
    y_c = (a_ref[...].astype(f32) * conv).astype(bf16)
    p_c = jnp.dot(y_c, wbc_ref[...], preferred_element_type=f32)
    o_cat = jnp.concatenate([o_ref[h] for h in range(N_HP_GROUP)], axis=1)
    y_a = o_cat * sza_ref[...]
    p_a = jnp.dot(y_a, wba_ref[...], preferred_element_type=f32)
    m = (sgc_ref[...].astype(f32) * p_c + sga_ref[...].astype(f32) * p_a).astype(bf16)
    out_ref[...] = x_ref[...] + jnp.dot(m, wo_ref[...], preferred_element_type=f32)


def _out_proj(x, u, a, o, sza, sgc, sga, conv_w, conv_b, wbc, wba, wo):
    B, S, _ = x.shape
    nt = S // TM_OUT
    rb = TM_OUT // BF16_ROWS
    sq = pl.Squeezed()
    const = lambda b, t: (0, 0)
    tok = lambda b, t: (b, t, 0)
    tok_spec = lambda w: pl.BlockSpec((sq, TM_OUT, w), tok)
    halo_prev = pl.BlockSpec((sq, BF16_ROWS, W_CONV), lambda b, t: (b, jnp.maximum(t * rb - 1, 0), 0))
    halo_next = pl.BlockSpec((sq, BF16_ROWS, W_CONV),
                             lambda b, t: (b, jnp.minimum((t + 1) * rb, S // BF16_ROWS - 1), 0))
    return pl.pallas_call(
        _out_proj_body,
        grid=(B, nt),
        in_specs=[
            tok_spec(D_MODEL), tok_spec(W_CONV), halo_prev, halo_next, tok_spec(W_CONV),
            pl.BlockSpec((sq, N_HP_GROUP, TM_OUT, HP_WIDTH), lambda b, t: (b, 0, t, 0)),
            tok_spec(W_ATTN_OUT), tok_spec(D_MODEL), tok_spec(D_MODEL),
            pl.BlockSpec((3, W_CONV), const), pl.BlockSpec((1, W_CONV), const),
            pl.BlockSpec((W_CONV, D_MODEL), const), pl.BlockSpec((W_ATTN_OUT, D_MODEL), const),
            pl.BlockSpec((D_MODEL, D_MODEL), const),
        ],
        out_specs=tok_spec(D_MODEL),
        out_shape=jax.ShapeDtypeStruct((B, S, D_MODEL), f32),
        compiler_params=pltpu.CompilerParams(
            dimension_semantics=("arbitrary", "arbitrary"),
            vmem_limit_bytes=VMEM_LIMIT_OUT),
        name="out_proj",
    )(x, u, u, u, a, o, sza, sgc, sga, conv_w, conv_b, wbc, wba, wo)


def _rotary_tables(S):
    half = HEAD_DIM // 2
    inv_freq = ROPE_THETA ** (-jnp.arange(0, half, dtype=f32) / half)
    ang = jnp.arange(S, dtype=f32)[:, None] * inv_freq[None, :]
    cos, sin = jnp.cos(ang), jnp.sin(ang)
    reps = HP_WIDTH // HEAD_DIM
    cos_t = jnp.tile(jnp.concatenate([cos, cos], axis=1), (1, reps))
    sin_t = jnp.tile(jnp.concatenate([-sin, sin], axis=1), (1, reps))
    return cos_t, sin_t


def kernel(x, norm_g, w_in, conv_w, conv_b, q_norm_g, k_norm_g, w_branch_conv, w_branch_attn, w_out):
    B, S, D = x.shape
    assert D == D_MODEL and w_in.shape == (D_MODEL, IN_WIDTH)
    assert S % TM_IN == 0 and S % TM_OUT == 0 and (S // DILATIONS[-1]) % BQ == 0
    assert (S // BQ) % 2 == 0 and (TM_IN // DILATIONS[-1]) % BF16_ROWS == 0
    cos_t, sin_t = _rotary_tables(S)
    reps = HP_WIDTH // HEAD_DIM
    qg = jnp.tile(q_norm_g.astype(f32), reps)[None, :]
    kg = jnp.tile(k_norm_g.astype(f32), reps)[None, :]
    head_id = np.arange(CHUNK) // HEAD_DIM
    ones_bd = jnp.asarray(head_id[:, None] == head_id[None, :], dtype=bf16)

    outs = _in_proj(x, norm_g.astype(f32)[None, :], w_in.astype(bf16), cos_t, sin_t, qg, kg, ones_bd)
    u, a = outs[0], outs[1]
    sza, sgc, sga = outs[11], outs[12], outs[13]
    o = _band_attn(outs[2:11])
    return _out_proj(x, u, a, o, sza, sgc, sga, conv_w.astype(f32), conv_b.astype(f32)[None, :],
                     w_branch_conv.astype(bf16), w_branch_attn.astype(bf16), w_out.astype(bf16))
```

```python
import math

import jax
import jax.numpy as jnp
import numpy as np
from jax import lax
from jax.experimental import pallas as pl
from jax.experimental.pallas import tpu as pltpu

D_MODEL = 1024
W_CONV = 1024
HEAD_DIM = 64
HEADS_PER_GROUP = 8
DILATIONS = (1, 4, 16)
BAND_HALF = 64
N_GROUPS = 3
W_QKV = N_GROUPS * HEADS_PER_GROUP * HEAD_DIM
W_ATTN_OUT = HEADS_PER_GROUP * HEAD_DIM
ROPE_THETA = 10000.0
NORM_EPS = 1e-6

OFF_B, OFF_C, OFF_H, OFF_ZC = 0, 1024, 2048, 3072
OFF_Q = 4096
OFF_K = OFF_Q + W_QKV
OFF_V = OFF_K + W_QKV
OFF_ZA = OFF_V + W_QKV
OFF_GC = OFF_ZA + W_ATTN_OUT
OFF_GA = OFF_GC + D_MODEL
IN_WIDTH = OFF_GA + D_MODEL

LANES = 128
MXU_WIDTH = 256
HP_WIDTH = 2 * HEAD_DIM
N_HP_GROUP = HEADS_PER_GROUP // 2
BF16_ROWS = 16

TM_IN = 512
TM_OUT = 512
CHUNK = MXU_WIDTH
HP_PER_CHUNK = CHUNK // HP_WIDTH
N_PERM = 4
BQ = 128
SPAN = BQ + 2 * BAND_HALF
PIPE_BLOCKS = 4
N_BIAS = SPAN // BAND_HALF - 1
NEG_BIG = -1e30
LOG2E = math.log2(math.e)

VMEM_LIMIT_IN = 58 * 1024 * 1024
VMEM_LIMIT_ATTN = 48 * 1024 * 1024
VMEM_LIMIT_OUT = 48 * 1024 * 1024

f32 = jnp.float32
bf16 = jnp.bfloat16


def _sigmoid(z):
    return 1.0 / (1.0 + jnp.exp(-z))


def _in_proj_body(x_ref, ng_ref, w_ref, cos_ref, sin_ref, qg_ref, kg_ref, ones_ref,
                  u_ref, a_ref, q0_ref, q1_ref, q2_ref, k0_ref, k1_ref, k2_ref,
                  v0_ref, v1_ref, v2_ref, sza_ref, sgc_ref, sga_ref, xn_ref, perm_ref, t_ref):
    x = x_ref[...]
    ms = jnp.mean(x * x, axis=-1, keepdims=True)
    xn_ref[...] = (x * lax.rsqrt(ms + NORM_EPS) * ng_ref[...]).astype(bf16)

    def proj(c0):
        return jnp.dot(xn_ref[...], w_ref[:, c0:c0 + CHUNK], preferred_element_type=f32)

    n_perm_used = [0]

    def store_head_pair(group_refs, hp, val):
        g, h = divmod(hp, N_HP_GROUP)
        d = DILATIONS[g]
        if d == 1:
            group_refs[g][h] = val.astype(bf16)
            return
        buf = perm_ref.at[n_perm_used[0] % N_PERM]
        n_perm_used[0] += 1
        buf[...] = val
        for r in range(d):
            group_refs[g][h, :, r * HP_WIDTH:(r + 1) * HP_WIDTH] = (
                buf[pl.ds(r, TM_IN // d, stride=d), :].astype(bf16))

    for j in range(0, W_CONV, CHUNK):
        u_ref[:, j:j + CHUNK] = (proj(OFF_C + j) * proj(OFF_H + j)).astype(bf16)
        z = proj(OFF_ZC + j)
        a_ref[:, j:j + CHUNK] = (proj(OFF_B + j) * z * _sigmoid(z)).astype(bf16)

    lane = lax.broadcasted_iota(jnp.int32, (1, HP_WIDTH), 1)
    first_half = (lane % HEAD_DIM) < (HEAD_DIM // 2)
    cos = cos_ref[...]
    sin = sin_ref[...]

    def qk_epilogue(t, g_ref, refs, scale, c):
        ss = jnp.dot((t * t).astype(bf16), ones_ref[...], preferred_element_type=f32)
        r = lax.rsqrt(ss * (1.0 / HEAD_DIM) + NORM_EPS) * scale
        for p in range(HP_PER_CHUNK):
            cols = slice(p * HP_WIDTH, (p + 1) * HP_WIDTH)
            tg = t[:, cols] * g_ref[...]
            partner = jnp.where(first_half,
                                pltpu.roll(tg, HP_WIDTH - HEAD_DIM // 2, axis=1),
                                pltpu.roll(tg, HEAD_DIM // 2, axis=1))
            store_head_pair(refs, c * HP_PER_CHUNK + p, (tg * cos + partner * sin) * r[:, cols])

    qk_chunks = [(OFF_Q + c * CHUNK, qg_ref, (q0_ref, q1_ref, q2_ref), LOG2E / math.sqrt(HEAD_DIM), c)
                 for c in range(W_QKV // CHUNK)]
    qk_chunks += [(OFF_K + c * CHUNK, kg_ref, (k0_ref, k1_ref, k2_ref), 1.0, c)
                  for c in range(W_QKV // CHUNK)]
    for i, ch in enumerate(qk_chunks):
        t_ref[i % 2] = proj(ch[0])
        if i > 0:
            qk_epilogue(t_ref[(i - 1) % 2], *qk_chunks[i - 1][1:])
    qk_epilogue(t_ref[(len(qk_chunks) - 1) % 2], *qk_chunks[-1][1:])

    for c in range(W_QKV // CHUNK):
        t = proj(OFF_V + c * CHUNK)
        for p in range(HP_PER_CHUNK):
            store_head_pair((v0_ref, v1_ref, v2_ref), c * HP_PER_CHUNK + p,
                            t[:, p * HP_WIDTH:(p + 1) * HP_WIDTH])

    for j in range(0, W_ATTN_OUT, CHUNK):
        z = proj(OFF_ZA + j)
        sza_ref[:, j:j + CHUNK] = (z * _sigmoid(z)).astype(bf16)
    for j in range(0, D_MODEL, CHUNK):
        sgc_ref[:, j:j + CHUNK] = _sigmoid(proj(OFF_GC + j)).astype(bf16)
        sga_ref[:, j:j + CHUNK] = _sigmoid(proj(OFF_GA + j)).astype(bf16)


def _in_proj(x, norm_g, w_in, cos_t, sin_t, qg, kg, ones_bd):
    B, S, _ = x.shape
    nt = S // TM_IN
    const = lambda b, t: (0, 0)
    tok = lambda b, t: (b, t, 0)
    sq = pl.Squeezed()
    tok_spec = lambda w: pl.BlockSpec((sq, TM_IN, w), tok)
    tok_shape = lambda w: jax.ShapeDtypeStruct((B, S, w), bf16)
    grp_shapes = tuple(jax.ShapeDtypeStruct((B, N_HP_GROUP, S // d, d * HP_WIDTH), bf16)
                       for d in DILATIONS)
    grp_specs = tuple(pl.BlockSpec((sq, N_HP_GROUP, TM_IN // d, d * HP_WIDTH),
                                   lambda b, t: (b, 0, t, 0)) for d in DILATIONS)
    return pl.pallas_call(
        _in_proj_body,
        grid=(B, nt),
        in_specs=[
            tok_spec(D_MODEL),
            pl.BlockSpec((1, D_MODEL), const),
            pl.BlockSpec((D_MODEL, IN_WIDTH), const, pipeline_mode=pl.Buffered(1)),
            pl.BlockSpec((TM_IN, HP_WIDTH), lambda b, t: (t, 0)),
            pl.BlockSpec((TM_IN, HP_WIDTH), lambda b, t: (t, 0)),
            pl.BlockSpec((1, HP_WIDTH), const),
            pl.BlockSpec((1, HP_WIDTH), const),
            pl.BlockSpec((CHUNK, CHUNK), const),
        ],
        out_specs=(tok_spec(W_CONV), tok_spec(W_CONV)) + grp_specs * 3
                  + (tok_spec(W_ATTN_OUT), tok_spec(D_MODEL), tok_spec(D_MODEL)),
        out_shape=(tok_shape(W_CONV), tok_shape(W_CONV)) + grp_shapes * 3
                  + (tok_shape(W_ATTN_OUT), tok_shape(D_MODEL), tok_shape(D_MODEL)),
        scratch_shapes=[pltpu.VMEM((TM_IN, D_MODEL), bf16),
                        pltpu.VMEM((N_PERM, TM_IN, HP_WIDTH), f32),
                        pltpu.VMEM((2, TM_IN, CHUNK), f32)],
        compiler_params=pltpu.CompilerParams(
            dimension_semantics=("arbitrary", "arbitrary"),
            vmem_limit_bytes=VMEM_LIMIT_IN),
        name="in_proj",
    )(x, norm_g, w_in, cos_t, sin_t, qg, kg, ones_bd)


def _scores(q, ks):
    lane = lax.broadcasted_iota(jnp.int32, (BQ, HP_WIDTH), 1)
    head0 = lane < HEAD_DIM
    zero = jnp.zeros_like(q)
    q2 = jnp.concatenate([jnp.where(head0, q, zero), jnp.where(head0, zero, q)], axis=0)
    return lax.dot_general(q2, ks, (((1,), (1,)), ((), ())), preferred_element_type=f32)


def _softmax_pv(s, vs, bias):
    lane = lax.broadcasted_iota(jnp.int32, (BQ, HP_WIDTH), 1)
    head0 = lane < HEAD_DIM
    s = s + bias
    m = jnp.max(s, axis=1, keepdims=True)
    p = jnp.exp2(s - m).astype(bf16)
    vs1 = jnp.concatenate([vs, jnp.ones_like(vs)], axis=1)
    pv = jnp.dot(p, vs1, preferred_element_type=f32)
    acc = jnp.where(head0, pv[:BQ, :HP_WIDTH], pv[BQ:, :HP_WIDTH])
    l_sel = jnp.where(head0, pv[:BQ, HP_WIDTH:], pv[BQ:, HP_WIDTH:])
    m_sel = jnp.where(head0, m[:BQ], m[BQ:])
    return acc * (1.0 / l_sel), m_sel + jnp.log2(l_sel)


def _band_attn_body(bias_ref, q0_ref, k0_ref, v0_ref, q1_ref, k1_ref, v1_ref,
                    q2_ref, k2_ref, v2_ref, o_ref, o0_s, l0_s, o1_s, l1_s, o2_s, l2_s, s_scr):
    def addr(n, L):
        if isinstance(n, int):
            qs = n * BQ
            st = min(max(qs - BAND_HALF, 0), L - SPAN)
            return qs, st, (qs - st) // BAND_HALF
        qs = pl.multiple_of(n * BQ, BQ)
        st = pl.multiple_of(jnp.clip(n * BQ - BAND_HALF, 0, L - SPAN), BAND_HALF)
        return qs, st, lax.shift_right_logical(qs - st, BAND_HALF.bit_length() - 1)

    def run_group(trips, blocks_of, store):
        def scores_stage(i, slot):
            for j, (q_ref, k_ref, _, cols, n, L, _) in enumerate(blocks_of(i)):
                qs, st, _ = addr(n, L)
                s_scr[slot, j] = _scores(q_ref[pl.ds(qs, BQ), cols], k_ref[pl.ds(st, SPAN), cols])

        def softmax_stage(i, slot):
            for j, (_, _, v_ref, cols, n, L, r) in enumerate(blocks_of(i)):
                qs, st, case = addr(n, L)
                o, lse = _softmax_pv(s_scr[slot, j], v_ref[pl.ds(st, SPAN), cols], bias_ref[case])
                store(qs, o, lse, r)

        assert trips % 2 == 0
        scores_stage(0, 0)

        def step(h, carry):
            i = 2 * h
            scores_stage(i + 1, 1)
            softmax_stage(i, 0)
            scores_stage(jnp.minimum(i + 2, trips - 1), 0)
            softmax_stage(i + 1, 1)
            return carry

        lax.fori_loop(0, trips // 2, step, 0)

    def make_store(o_s, l_s, d):
        def store(qs, o, lse, r):
            rows = pl.ds(qs * d + r, BQ, stride=d) if d > 1 else pl.ds(qs, BQ)
            o_s[rows, :] = o
            l_s[rows, :] = lse
        return store

    def blocks_of_group(q_ref, k_ref, v_ref, d):
        L = q_ref.shape[0]
        nb = L // BQ
        if d == 1:
            return nb // PIPE_BLOCKS, lambda i: [
                (q_ref, k_ref, v_ref, slice(None), PIPE_BLOCKS * i + j, L, 0) for j in range(PIPE_BLOCKS)]
        if d <= PIPE_BLOCKS:
            return nb, lambda i: [
                (q_ref, k_ref, v_ref, slice(r * HP_WIDTH, (r + 1) * HP_WIDTH), i, L, r) for r in range(d)]
        per_iter = PIPE_BLOCKS // nb

        def blocks(i):
            out = []
            for jr in range(per_iter):
                r = i * per_iter + jr
                cols = pl.ds(pl.multiple_of(r * HP_WIDTH, HP_WIDTH), HP_WIDTH)
                out += [(q_ref, k_ref, v_ref, cols, n, L, r) for n in range(nb)]
            return out
        return d // per_iter, blocks

    for (q_ref, k_ref, v_ref), o_s, l_s, d in (((q0_ref, k0_ref, v0_ref), o0_s, l0_s, DILATIONS[0]),
                                               ((q1_ref, k1_ref, v1_ref), o1_s, l1_s, DILATIONS[1]),
                                               ((q2_ref, k2_ref, v2_ref), o2_s, l2_s, DILATIONS[2])):
        trips, blocks = blocks_of_group(q_ref, k_ref, v_ref, d)
        run_group(trips, blocks, make_store(o_s, l_s, d))

    def merge(c, carry):
        rows = pl.ds(pl.multiple_of(c * BQ, BQ), BQ)
        lse0, lse1, lse2 = l0_s[rows, :], l1_s[rows, :], l2_s[rows, :]
        mx = jnp.maximum(jnp.maximum(lse0, lse1), lse2)
        w0, w1, w2 = jnp.exp2(lse0 - mx), jnp.exp2(lse1 - mx), jnp.exp2(lse2 - mx)
        o = (w0 * o0_s[rows, :] + w1 * o1_s[rows, :] + w2 * o2_s[rows, :]) * (1.0 / (w0 + w1 + w2))
        o_ref[rows, :] = o.astype(bf16)
        return carry

    lax.fori_loop(0, o_ref.shape[0] // BQ, merge, 0)


def _band_bias():
    qi = np.arange(2 * BQ)[:, None] % BQ
    kj = np.arange(SPAN)[None, :]
    rel = np.stack([kj - qi - c * BAND_HALF for c in range(N_BIAS)])
    return jnp.asarray(np.where(np.abs(rel) <= BAND_HALF, 0.0, NEG_BIG), dtype=f32)


def _band_attn(qkv):
    B = qkv[0].shape[0]
    S = qkv[0].shape[2]
    sq = pl.Squeezed()
    hp_block = lambda b, h: (b, h, 0, 0)
    operands, specs = [], []
    for g, d in enumerate(DILATIONS):
        for a in range(3):
            operands.append(qkv[a * N_GROUPS + g])
            specs.append(pl.BlockSpec((sq, sq, S // d, d * HP_WIDTH), hp_block))
    return pl.pallas_call(
        _band_attn_body,
        grid=(B, N_HP_GROUP),
        in_specs=[pl.BlockSpec((N_BIAS, 2 * BQ, SPAN), lambda b, h: (0, 0, 0))] + specs,
        out_specs=pl.BlockSpec((sq, sq, S, HP_WIDTH), hp_block),
        out_shape=jax.ShapeDtypeStruct((B, N_HP_GROUP, S, HP_WIDTH), bf16),
        scratch_shapes=[pltpu.VMEM((S, HP_WIDTH), f32) for _ in range(2 * N_GROUPS)]
                       + [pltpu.VMEM((2, PIPE_BLOCKS, 2 * BQ, SPAN), f32)],
        compiler_params=pltpu.CompilerParams(
            dimension_semantics=("arbitrary", "arbitrary"),
            vmem_limit_bytes=VMEM_LIMIT_ATTN),
        name="band_attn",
    )(_band_bias(), *operands)


def _out_proj_body(x_ref, u_ref, up_ref, un_ref, a_ref, o_ref, sza_ref, sgc_ref, sga_ref,
                   cw_ref, cb_ref, wbc_ref, wba_ref, wo_ref, out_ref):
    t = pl.program_id(1)
    nt = pl.num_programs(1)
    u = u_ref[...].astype(f32)
    row = lax.broadcasted_iota(jnp.int32, (TM_OUT, 1), 0)
    prev_row = up_ref[...].astype(f32)[BF16_ROWS - 1:BF16_ROWS] * jnp.where(t > 0, 1.0, 0.0)
    next_row = un_ref[...].astype(f32)[0:1] * jnp.where(t < nt - 1, 1.0, 0.0)
    u_prev = jnp.where(row == 0, prev_row, pltpu.roll(u, 1, axis=0))
    u_next = jnp.where(row == TM_OUT - 1, next_row, pltpu.roll(u, TM_OUT - 1, axis=0))
    conv = u_prev * cw_ref[0:1, :] + u * cw_ref[1:2, :] + u_next * cw_ref[2:3, :] + cb_ref[...]
    y_c = (a_ref[...].astype(f32) * conv).astype(bf16)
    p_c = jnp.dot(y_c, wbc_ref[...], preferred_element_type=f32)
    o_cat = jnp.concatenate([o_ref[h] for h in range(N_HP_GROUP)], axis=1)
    y_a = o_cat * sza_ref[...]
    p_a = jnp.dot(y_a, wba_ref[...], preferred_element_type=f32)
    m = (sgc_ref[...].astype(f32) * p_c + sga_ref[...].astype(f32) * p_a).astype(bf16)
    out_ref[...] = x_ref[...] + jnp.dot(m, wo_ref[...], preferred_element_type=f32)


def _out_proj(x, u, a, o, sza, sgc, sga, conv_w, conv_b, wbc, wba, wo):
    B, S, _ = x.shape
    nt = S // TM_OUT
    rb = TM_OUT // BF16_ROWS
    sq = pl.Squeezed()
    const = lambda b, t: (0, 0)
    tok = lambda b, t: (b, t, 0)
    tok_spec = lambda w: pl.BlockSpec((sq, TM_OUT, w), tok)
    halo_prev = pl.BlockSpec((sq, BF16_ROWS, W_CONV), lambda b, t: (b, jnp.maximum(t * rb - 1, 0), 0))
    halo_next = pl.BlockSpec((sq, BF16_ROWS, W_CONV),
                             lambda b, t: (b, jnp.minimum((t + 1) * rb, S // BF16_ROWS - 1), 0))
    return pl.pallas_call(
        _out_proj_body,
        grid=(B, nt),
        in_specs=[
            tok_spec(D_MODEL), tok_spec(W_CONV), halo_prev, halo_next, tok_spec(W_CONV),
            pl.BlockSpec((sq, N_HP_GROUP, TM_OUT, HP_WIDTH), lambda b, t: (b, 0, t, 0)),
            tok_spec(W_ATTN_OUT), tok_spec(D_MODEL), tok_spec(D_MODEL),
            pl.BlockSpec((3, W_CONV), const), pl.BlockSpec((1, W_CONV), const),
            pl.BlockSpec((W_CONV, D_MODEL), const), pl.BlockSpec((W_ATTN_OUT, D_MODEL), const),
            pl.BlockSpec((D_MODEL, D_MODEL), const),
        ],
        out_specs=tok_spec(D_MODEL),
        out_shape=jax.ShapeDtypeStruct((B, S, D_MODEL), f32),
        compiler_params=pltpu.CompilerParams(
            dimension_semantics=("arbitrary", "arbitrary"),
            vmem_limit_bytes=VMEM_LIMIT_OUT),
        name="out_proj",
    )(x, u, u, u, a, o, sza, sgc, sga, conv_w, conv_b, wbc, wba, wo)


def _rotary_tables(S):
    half = HEAD_DIM // 2
    inv_freq = ROPE_THETA ** (-jnp.arange(0, half, dtype=f32) / half)
    ang = jnp.arange(S, dtype=f32)[:, None] * inv_freq[None, :]
    cos, sin = jnp.cos(ang), jnp.sin(ang)
    reps = HP_WIDTH // HEAD_DIM
    cos_t = jnp.tile(jnp.concatenate([cos, cos], axis=1), (1, reps))
    sin_t = jnp.tile(jnp.concatenate([-sin, sin], axis=1), (1, reps))
    return cos_t, sin_t


def kernel(x, norm_g, w_in, conv_w, conv_b, q_norm_g, k_norm_g, w_branch_conv, w_branch_attn, w_out):
    B, S, D = x.shape
    assert D == D_MODEL and w_in.shape == (D_MODEL, IN_WIDTH)
    assert S % TM_IN == 0 and S % TM_OUT == 0 and (S // DILATIONS[-1]) % BQ == 0
    assert (TM_IN // DILATIONS[-1]) % BF16_ROWS == 0 and (S // BQ) % PIPE_BLOCKS == 0
    cos_t, sin_t = _rotary_tables(S)
    reps = HP_WIDTH // HEAD_DIM
    qg = jnp.tile(q_norm_g.astype(f32), reps)[None, :]
    kg = jnp.tile(k_norm_g.astype(f32), reps)[None, :]
    head_id = np.arange(CHUNK) // HEAD_DIM
    ones_bd = jnp.asarray(head_id[:, None] == head_id[None, :], dtype=bf16)

    outs = _in_proj(x, norm_g.astype(f32)[None, :], w_in.astype(bf16), cos_t, sin_t, qg, kg, ones_bd)
    u, a = outs[0], outs[1]
    sza, sgc, sga = outs[11], outs[12], outs[13]
    o = _band_attn(outs[2:11])
    return _out_proj(x, u, a, o, sza, sgc, sga, conv_w.astype(f32), conv_b.astype(f32)[None, :],
                     w_branch_conv.astype(bf16), w_branch_attn.astype(bf16), w_out.astype(bf16))
```

```python
import math

import jax
import jax.numpy as jnp
import numpy as np
from jax import lax
from jax.experimental import pallas as pl
from jax.experimental.pallas import tpu as pltpu

D_MODEL = 1024
W_CONV = 1024
HEAD_DIM = 64
HEADS_PER_GROUP = 8
DILATIONS = (1, 4, 16)
BAND_HALF = 64
N_GROUPS = 3
W_QKV = N_GROUPS * HEADS_PER_GROUP * HEAD_DIM
W_ATTN_OUT = HEADS_PER_GROUP * HEAD_DIM
ROPE_THETA = 10000.0
NORM_EPS = 1e-6

OFF_B, OFF_C, OFF_H, OFF_ZC = 0, 1024, 2048, 3072
OFF_Q = 4096
OFF_K = OFF_Q + W_QKV
OFF_V = OFF_K + W_QKV
OFF_ZA = OFF_V + W_QKV
OFF_GC = OFF_ZA + W_ATTN_OUT
OFF_GA = OFF_GC + D_MODEL
IN_WIDTH = OFF_GA + D_MODEL

LANES = 128
MXU_WIDTH = 256
HP_WIDTH = 2 * HEAD_DIM
N_HP_GROUP = HEADS_PER_GROUP // 2
F32_ROWS = 8
BF16_ROWS = 16

TM_IN = 512
TM_OUT = 512
CHUNK = MXU_WIDTH
HP_PER_CHUNK = CHUNK // HP_WIDTH
N_PERM = 4
BQ = 128
SPAN = BQ + 2 * BAND_HALF
RES_PAD = 8
PIPE_BLOCKS = 4
N_BIAS = SPAN // BAND_HALF - 1
NEG_BIG = -1e30
LOG2E = math.log2(math.e)

VMEM_LIMIT_IN = 58 * 1024 * 1024
VMEM_LIMIT_ATTN = 48 * 1024 * 1024
VMEM_LIMIT_OUT = 48 * 1024 * 1024

f32 = jnp.float32
bf16 = jnp.bfloat16


def _sigmoid(z):
    return 1.0 / (1.0 + jnp.exp(-z))


def _in_proj_body(x_ref, ng_ref, w_ref, cos_ref, sin_ref, qg_ref, kg_ref, ones_ref,
                  u_ref, a_ref, q0_ref, q1_ref, q2_ref, k0_ref, k1_ref, k2_ref,
                  v0_ref, v1_ref, v2_ref, sza_ref, sgc_ref, sga_ref, xn_ref, perm_ref, t_ref):
    x = x_ref[...]
    ms = jnp.mean(x * x, axis=-1, keepdims=True)
    xn_ref[...] = (x * lax.rsqrt(ms + NORM_EPS) * ng_ref[...]).astype(bf16)

    def proj(c0):
        return jnp.dot(xn_ref[...], w_ref[:, c0:c0 + CHUNK], preferred_element_type=f32)

    n_perm_used = [0]

    def store_head_pair(group_refs, hp, val):
        g, h = divmod(hp, N_HP_GROUP)
        d = DILATIONS[g]
        if d == 1:
            group_refs[g][h] = val.astype(bf16)
            return
        buf = perm_ref.at[n_perm_used[0] % N_PERM]
        n_perm_used[0] += 1
        buf[...] = val
        for r in range(d):
            group_refs[g][h, :, r * HP_WIDTH:(r + 1) * HP_WIDTH] = (
                buf[pl.ds(r, TM_IN // d, stride=d), :].astype(bf16))

    for j in range(0, W_CONV, CHUNK):
        u_ref[:, j:j + CHUNK] = (proj(OFF_C + j) * proj(OFF_H + j)).astype(bf16)
        z = proj(OFF_ZC + j)
        a_ref[:, j:j + CHUNK] = (proj(OFF_B + j) * z * _sigmoid(z)).astype(bf16)

    lane = lax.broadcasted_iota(jnp.int32, (1, HP_WIDTH), 1)
    first_half = (lane % HEAD_DIM) < (HEAD_DIM // 2)
    cos = cos_ref[...]
    sin = sin_ref[...]

    def qk_epilogue(t, g_ref, refs, scale, c):
        ss = jnp.dot((t * t).astype(bf16), ones_ref[...], preferred_element_type=f32)
        r = lax.rsqrt(ss * (1.0 / HEAD_DIM) + NORM_EPS) * scale
        for p in range(HP_PER_CHUNK):
            cols = slice(p * HP_WIDTH, (p + 1) * HP_WIDTH)
            tg = t[:, cols] * g_ref[...]
            partner = jnp.where(first_half,
                                pltpu.roll(tg, HP_WIDTH - HEAD_DIM // 2, axis=1),
                                pltpu.roll(tg, HEAD_DIM // 2, axis=1))
            store_head_pair(refs, c * HP_PER_CHUNK + p, (tg * cos + partner * sin) * r[:, cols])

    qk_chunks = [(OFF_Q + c * CHUNK, qg_ref, (q0_ref, q1_ref, q2_ref), LOG2E / math.sqrt(HEAD_DIM), c)
                 for c in range(W_QKV // CHUNK)]
    qk_chunks += [(OFF_K + c * CHUNK, kg_ref, (k0_ref, k1_ref, k2_ref), 1.0, c)
                  for c in range(W_QKV // CHUNK)]
    for i, ch in enumerate(qk_chunks):
        t_ref[i % 2] = proj(ch[0])
        if i > 0:
            qk_epilogue(t_ref[(i - 1) % 2], *qk_chunks[i - 1][1:])
    qk_epilogue(t_ref[(len(qk_chunks) - 1) % 2], *qk_chunks[-1][1:])

    for c in range(W_QKV // CHUNK):
        t = proj(OFF_V + c * CHUNK)
        for p in range(HP_PER_CHUNK):
            store_head_pair((v0_ref, v1_ref, v2_ref), c * HP_PER_CHUNK + p,
                            t[:, p * HP_WIDTH:(p + 1) * HP_WIDTH])

    for j in range(0, W_ATTN_OUT, CHUNK):
        z = proj(OFF_ZA + j)
        sza_ref[:, j:j + CHUNK] = (z * _sigmoid(z)).astype(bf16)
    for j in range(0, D_MODEL, CHUNK):
        sgc_ref[:, j:j + CHUNK] = _sigmoid(proj(OFF_GC + j)).astype(bf16)
        sga_ref[:, j:j + CHUNK] = _sigmoid(proj(OFF_GA + j)).astype(bf16)


def _in_proj(x, norm_g, w_in, cos_t, sin_t, qg, kg, ones_bd):
    B, S, _ = x.shape
    nt = S // TM_IN
    const = lambda b, t: (0, 0)
    tok = lambda b, t: (b, t, 0)
    sq = pl.Squeezed()
    tok_spec = lambda w: pl.BlockSpec((sq, TM_IN, w), tok)
    tok_shape = lambda w: jax.ShapeDtypeStruct((B, S, w), bf16)
    grp_shapes = tuple(jax.ShapeDtypeStruct((B, N_HP_GROUP, S // d, d * HP_WIDTH), bf16)
                       for d in DILATIONS)
    grp_specs = tuple(pl.BlockSpec((sq, N_HP_GROUP, TM_IN // d, d * HP_WIDTH),
                                   lambda b, t: (b, 0, t, 0)) for d in DILATIONS)
    return pl.pallas_call(
        _in_proj_body,
        grid=(B, nt),
        in_specs=[
            tok_spec(D_MODEL),
            pl.BlockSpec((1, D_MODEL), const),
            pl.BlockSpec((D_MODEL, IN_WIDTH), const, pipeline_mode=pl.Buffered(1)),
            pl.BlockSpec((TM_IN, HP_WIDTH), lambda b, t: (t, 0)),
            pl.BlockSpec((TM_IN, HP_WIDTH), lambda b, t: (t, 0)),
            pl.BlockSpec((1, HP_WIDTH), const),
            pl.BlockSpec((1, HP_WIDTH), const),
            pl.BlockSpec((CHUNK, CHUNK), const),
        ],
        out_specs=(tok_spec(W_CONV), tok_spec(W_CONV)) + grp_specs * 3
                  + (tok_spec(W_ATTN_OUT), tok_spec(D_MODEL), tok_spec(D_MODEL)),
        out_shape=(tok_shape(W_CONV), tok_shape(W_CONV)) + grp_shapes * 3
                  + (tok_shape(W_ATTN_OUT), tok_shape(D_MODEL), tok_shape(D_MODEL)),
        scratch_shapes=[pltpu.VMEM((TM_IN, D_MODEL), bf16),
                        pltpu.VMEM((N_PERM, TM_IN, HP_WIDTH), f32),
                        pltpu.VMEM((2, TM_IN, CHUNK), f32)],
        compiler_params=pltpu.CompilerParams(
            dimension_semantics=("arbitrary", "arbitrary"),
            vmem_limit_bytes=VMEM_LIMIT_IN),
        name="in_proj",
    )(x, norm_g, w_in, cos_t, sin_t, qg, kg, ones_bd)


def _scores(q, ks):
    lane = lax.broadcasted_iota(jnp.int32, (BQ, HP_WIDTH), 1)
    head0 = lane < HEAD_DIM
    zero = jnp.zeros_like(q)
    q2 = jnp.concatenate([jnp.where(head0, q, zero), jnp.where(head0, zero, q)], axis=0)
    return lax.dot_general(q2, ks, (((1,), (1,)), ((), ())), preferred_element_type=f32)


def _softmax_pv(s, vs, bias):
    lane = lax.broadcasted_iota(jnp.int32, (BQ, HP_WIDTH), 1)
    head0 = lane < HEAD_DIM
    s = s + bias
    m = jnp.max(s, axis=1, keepdims=True)
    p = jnp.exp2(s - m).astype(bf16)
    vs1 = jnp.concatenate([vs, jnp.ones_like(vs)], axis=1)
    pv = jnp.dot(p, vs1, preferred_element_type=f32)
    acc = jnp.where(head0, pv[:BQ, :HP_WIDTH], pv[BQ:, :HP_WIDTH])
    l_sel = jnp.where(head0, pv[:BQ, HP_WIDTH:], pv[BQ:, HP_WIDTH:])
    m_sel = jnp.where(head0, m[:BQ], m[BQ:])
    return acc * (1.0 / l_sel), m_sel + jnp.log2(l_sel)


def _band_attn_body(bias_ref, q0_ref, k0_ref, v0_ref, q1_ref, k1_ref, v1_ref,
                    q2_ref, k2_ref, v2_ref, o_ref, o0_s, l0_s, o1_s, l1_s, o2_s, l2_s, s_scr):
    def addr(n, L):
        if isinstance(n, int):
            qs = n * BQ
            st = min(max(qs - BAND_HALF, 0), L - SPAN)
            return qs, st, (qs - st) // BAND_HALF
        qs = pl.multiple_of(n * BQ, BQ)
        st = pl.multiple_of(jnp.clip(n * BQ - BAND_HALF, 0, L - SPAN), BAND_HALF)
        return qs, st, lax.shift_right_logical(qs - st, BAND_HALF.bit_length() - 1)

    def scores_stage(blocks, slot):
        for j, (q_ref, k_ref, _, cols, n, L, _) in enumerate(blocks):
            qs, st, _ = addr(n, L)
            s_scr[slot, j] = _scores(q_ref[pl.ds(qs, BQ), cols], k_ref[pl.ds(st, SPAN), cols])

    def softmax_stage(blocks, slot, store):
        for j, (_, _, v_ref, cols, n, L, r) in enumerate(blocks):
            qs, st, case = addr(n, L)
            o, lse = _softmax_pv(s_scr[slot, j], v_ref[pl.ds(st, SPAN), cols], bias_ref[case])
            store(qs, o, lse, r)

    def run_group(trips, blocks_of, store, next_blocks):
        assert trips % 2 == 0

        def step(h, carry):
            i = 2 * h
            scores_stage(blocks_of(i + 1), 1)
            softmax_stage(blocks_of(i), 0, store)
            scores_stage(blocks_of(i + 2), 0)
            softmax_stage(blocks_of(i + 1), 1, store)
            return carry

        lax.fori_loop(0, trips // 2 - 1, step, 0)
        scores_stage(blocks_of(trips - 1), 1)
        softmax_stage(blocks_of(trips - 2), 0, store)
        if next_blocks is not None:
            scores_stage(next_blocks, 0)
        softmax_stage(blocks_of(trips - 1), 1, store)

    d_last = DILATIONS[-1]
    pitch = o_ref.shape[0] // d_last + RES_PAD

    def make_store(o_s, l_s, d):
        def store(qs, o, lse, r):
            if d == 1:
                rows = pl.ds(qs, BQ)
            elif d == d_last:
                rows = pl.ds(pl.multiple_of(r * pitch + qs, 8), BQ)
            else:
                rows = pl.ds(qs * d + r, BQ, stride=d)
            o_s[rows, :] = o
            l_s[rows, :] = lse
        return store

    def blocks_of_group(q_ref, k_ref, v_ref, d):
        L = q_ref.shape[0]
        nb = L // BQ
        if d == 1:
            return nb // PIPE_BLOCKS, lambda i: [
                (q_ref, k_ref, v_ref, slice(None), PIPE_BLOCKS * i + j, L, 0) for j in range(PIPE_BLOCKS)]
        if d <= PIPE_BLOCKS:
            return nb, lambda i: [
                (q_ref, k_ref, v_ref, slice(r * HP_WIDTH, (r + 1) * HP_WIDTH), i, L, r) for r in range(d)]
        per_iter = PIPE_BLOCKS // nb

        def blocks(i):
            out = []
            for jr in range(per_iter):
                r = i * per_iter + jr
                cols = pl.ds(pl.multiple_of(r * HP_WIDTH, HP_WIDTH), HP_WIDTH)
                out += [(q_ref, k_ref, v_ref, cols, n, L, r) for n in range(nb)]
            return out
        return d // per_iter, blocks

    groups = []
    for (q_ref, k_ref, v_ref), o_s, l_s, d in (((q0_ref, k0_ref, v0_ref), o0_s, l0_s, DILATIONS[0]),
                                               ((q1_ref, k1_ref, v1_ref), o1_s, l1_s, DILATIONS[1]),
                                               ((q2_ref, k2_ref, v2_ref), o2_s, l2_s, DILATIONS[2])):
        trips, blocks = blocks_of_group(q_ref, k_ref, v_ref, d)
        groups.append((trips, blocks, make_store(o_s, l_s, d)))
    scores_stage(groups[0][1](0), 0)
    for g, (trips, blocks, store) in enumerate(groups):
        run_group(trips, blocks, store, groups[g + 1][1](0) if g + 1 < len(groups) else None)

    def merge(c, carry):
        rows = pl.ds(pl.multiple_of(c * BQ, BQ), BQ)

        def last_group(ref):
            base = c * (BQ // d_last)
            return jnp.concatenate([ref[pl.ds(base + a, d_last, stride=pitch), :]
                                    for a in range(BQ // d_last)], axis=0)

        lse0, lse1, lse2 = l0_s[rows, :], l1_s[rows, :], last_group(l2_s)
        mx = jnp.maximum(jnp.maximum(lse0, lse1), lse2)
        w0, w1, w2 = jnp.exp2(lse0 - mx), jnp.exp2(lse1 - mx), jnp.exp2(lse2 - mx)
        o = (w0 * o0_s[rows, :] + w1 * o1_s[rows, :] + w2 * last_group(o2_s)) * (1.0 / (w0 + w1 + w2))
        o_ref[rows, :] = o.astype(bf16)
        return carry

    lax.fori_loop(0, o_ref.shape[0] // BQ, merge, 0, unroll=2)


def _band_bias():
    qi = np.arange(2 * BQ)[:, None] % BQ
    kj = np.arange(SPAN)[None, :]
    rel = np.stack([kj - qi - c * BAND_HALF for c in range(N_BIAS)])
    return jnp.asarray(np.where(np.abs(rel) <= BAND_HALF, 0.0, NEG_BIG), dtype=f32)


def _band_attn(qkv):
    B = qkv[0].shape[0]
    S = qkv[0].shape[2]
    sq = pl.Squeezed()
    hp_block = lambda b, h: (b, h, 0, 0)
    operands, specs = [], []
    for g, d in enumerate(DILATIONS):
        for a in range(3):
            operands.append(qkv[a * N_GROUPS + g])
            specs.append(pl.BlockSpec((sq, sq, S // d, d * HP_WIDTH), hp_block))
    return pl.pallas_call(
        _band_attn_body,
        grid=(B, N_HP_GROUP),
        in_specs=[pl.BlockSpec((N_BIAS, 2 * BQ, SPAN), lambda b, h: (0, 0, 0))] + specs,
        out_specs=pl.BlockSpec((sq, sq, S, HP_WIDTH), hp_block),
        out_shape=jax.ShapeDtypeStruct((B, N_HP_GROUP, S, HP_WIDTH), bf16),
        scratch_shapes=[pltpu.VMEM((S, HP_WIDTH), f32) for _ in range(2 * N_GROUPS - 2)]
                       + [pltpu.VMEM((S + DILATIONS[-1] * RES_PAD, HP_WIDTH), f32) for _ in range(2)]
                       + [pltpu.VMEM((2, PIPE_BLOCKS, 2 * BQ, SPAN), f32)],
        compiler_params=pltpu.CompilerParams(
            dimension_semantics=("arbitrary", "arbitrary"),
            vmem_limit_bytes=VMEM_LIMIT_ATTN),
        name="band_attn",
    )(_band_bias(), *operands)


def _out_proj_body(x_ref, u_ref, up_ref, un_ref, a_ref, o_ref, sza_ref, sgc_ref, sga_ref,
                   cw_ref, cb_ref, wbc_ref, wba_ref, wo_ref, out_ref):
    t = pl.program_id(1)
    nt = pl.num_programs(1)
    u = u_ref[...].astype(f32)
    prev_row = up_ref[...].astype(f32)[BF16_ROWS - 1:BF16_ROWS] * jnp.where(t > 0, 1.0, 0.0)
    next_row = un_ref[...].astype(f32)[0:1] * jnp.where(t < nt - 1, 1.0, 0.0)
    row = lax.broadcasted_iota(jnp.int32, (F32_ROWS, 1), 0)
    rolled = pltpu.roll(u, 1, axis=0)
    u_prev = jnp.concatenate([jnp.where(row == 0, prev_row, rolled[:F32_ROWS]), rolled[F32_ROWS:]], axis=0)
    rolled = pltpu.roll(u, TM_OUT - 1, axis=0)
    u_next = jnp.concatenate([rolled[:-F32_ROWS],
                              jnp.where(row == F32_ROWS - 1, next_row, rolled[-F32_ROWS:])], axis=0)
    conv = u_prev * cw_ref[0:1, :] + u * cw_ref[1:2, :] + u_next * cw_ref[2:3, :] + cb_ref[...]
    y_c = (a_ref[...].astype(f32) * conv).astype(bf16)
    p_c = jnp.dot(y_c, wbc_ref[...], preferred_element_type=f32)
    o_cat = jnp.concatenate([o_ref[h] for h in range(N_HP_GROUP)], axis=1)
    y_a = o_cat * sza_ref[...]
    p_a = jnp.dot(y_a, wba_ref[...], preferred_element_type=f32)
    m = (sgc_ref[...].astype(f32) * p_c + sga_ref[...].astype(f32) * p_a).astype(bf16)
    out_ref[...] = x_ref[...] + jnp.dot(m, wo_ref[...], preferred_element_type=f32)


def _out_proj(x, u, a, o, sza, sgc, sga, conv_w, conv_b, wbc, wba, wo):
    B, S, _ = x.shape
    nt = S // TM_OUT
    rb = TM_OUT // BF16_ROWS
    sq = pl.Squeezed()
    const = lambda b, t: (0, 0)
    tok = lambda b, t: (b, t, 0)
    tok_spec = lambda w: pl.BlockSpec((sq, TM_OUT, w), tok)
    halo_prev = pl.BlockSpec((sq, BF16_ROWS, W_CONV), lambda b, t: (b, jnp.maximum(t * rb - 1, 0), 0))
    halo_next = pl.BlockSpec((sq, BF16_ROWS, W_CONV),
                             lambda b, t: (b, jnp.minimum((t + 1) * rb, S // BF16_ROWS - 1), 0))
    return pl.pallas_call(
        _out_proj_body,
        grid=(B, nt),
        in_specs=[
            tok_spec(D_MODEL), tok_spec(W_CONV), halo_prev, halo_next, tok_spec(W_CONV),
            pl.BlockSpec((sq, N_HP_GROUP, TM_OUT, HP_WIDTH), lambda b, t: (b, 0, t, 0)),
            tok_spec(W_ATTN_OUT), tok_spec(D_MODEL), tok_spec(D_MODEL),
            pl.BlockSpec((3, W_CONV), const), pl.BlockSpec((1, W_CONV), const),
            pl.BlockSpec((W_CONV, D_MODEL), const), pl.BlockSpec((W_ATTN_OUT, D_MODEL), const),
            pl.BlockSpec((D_MODEL, D_MODEL), const),
        ],
        out_specs=tok_spec(D_MODEL),
        out_shape=jax.ShapeDtypeStruct((B, S, D_MODEL), f32),
        compiler_params=pltpu.CompilerParams(
            dimension_semantics=("arbitrary", "arbitrary"),
            vmem_limit_bytes=VMEM_LIMIT_OUT),
        name="out_proj",
    )(x, u, u, u, a, o, sza, sgc, sga, conv_w, conv_b, wbc, wba, wo)


def _rotary_tables(S):
    half = HEAD_DIM // 2
    inv_freq = ROPE_THETA ** (-jnp.arange(0, half, dtype=f32) / half)
    ang = jnp.arange(S, dtype=f32)[:, None] * inv_freq[None, :]
    cos, sin = jnp.cos(ang), jnp.sin(ang)
    reps = HP_WIDTH // HEAD_DIM
    cos_t = jnp.tile(jnp.concatenate([cos, cos], axis=1), (1, reps))
    sin_t = jnp.tile(jnp.concatenate([-sin, sin], axis=1), (1, reps))
    return cos_t, sin_t


def kernel(x, norm_g, w_in, conv_w, conv_b, q_norm_g, k_norm_g, w_branch_conv, w_branch_attn, w_out):
    B, S, D = x.shape
    assert D == D_MODEL and w_in.shape == (D_MODEL, IN_WIDTH)
    assert S % TM_IN == 0 and S % TM_OUT == 0 and (S // DILATIONS[-1]) % BQ == 0
    assert (TM_IN // DILATIONS[-1]) % BF16_ROWS == 0 and (S // BQ) % PIPE_BLOCKS == 0
    cos_t, sin_t = _rotary_tables(S)
    reps = HP_WIDTH // HEAD_DIM
    qg = jnp.tile(q_norm_g.astype(f32), reps)[None, :]
    kg = jnp.tile(k_norm_g.astype(f32), reps)[None, :]
    head_id = np.arange(CHUNK) // HEAD_DIM
    ones_bd = jnp.asarray(head_id[:, None] == head_id[None, :], dtype=bf16)

    outs = _in_proj(x, norm_g.astype(f32)[None, :], w_in.astype(bf16), cos_t, sin_t, qg, kg, ones_bd)
    u, a = outs[0], outs[1]
    sza, sgc, sga = outs[11], outs[12], outs[13]
    o = _band_attn(outs[2:11])
    return _out_proj(x, u, a, o, sza, sgc, sga, conv_w.astype(f32), conv_b.astype(f32)[None, :],
                     w_branch_conv.astype(bf16), w_branch_attn.astype(bf16), w_out.astype(bf16))
```

```python
import math

import jax
import jax.numpy as jnp
import numpy as np
from jax import lax
from jax.experimental import pallas as pl
from jax.experimental.pallas import tpu as pltpu

D_MODEL = 1024
W_CONV = 1024
HEAD_DIM = 64
HEADS_PER_GROUP = 8
DILATIONS = (1, 4, 16)
BAND_HALF = 64
N_GROUPS = 3
W_QKV = N_GROUPS * HEADS_PER_GROUP * HEAD_DIM
W_ATTN_OUT = HEADS_PER_GROUP * HEAD_DIM
ROPE_THETA = 10000.0
NORM_EPS = 1e-6

OFF_B, OFF_C, OFF_H, OFF_ZC = 0, 1024, 2048, 3072
OFF_Q = 4096
OFF_K = OFF_Q + W_QKV
OFF_V = OFF_K + W_QKV
OFF_ZA = OFF_V + W_QKV
OFF_GC = OFF_ZA + W_ATTN_OUT
OFF_GA = OFF_GC + D_MODEL
IN_WIDTH = OFF_GA + D_MODEL

LANES = 128
MXU_WIDTH = 256
HP_WIDTH = 2 * HEAD_DIM
N_HP_GROUP = HEADS_PER_GROUP // 2
F32_ROWS = 8
BF16_ROWS = 16

TM_IN = 512
TM_OUT = 512
CHUNK = MXU_WIDTH
HP_PER_CHUNK = CHUNK // HP_WIDTH
N_PERM = 4
BQ = 128
SPAN = BQ + 2 * BAND_HALF
RES_PAD = 8
STEP_ITERS = 4
PIPE_BLOCKS = 4
N_BIAS = SPAN // BAND_HALF - 1
NEG_BIG = -1e30
LOG2E = math.log2(math.e)

VMEM_LIMIT_IN = 58 * 1024 * 1024
VMEM_LIMIT_ATTN = 48 * 1024 * 1024
VMEM_LIMIT_OUT = 48 * 1024 * 1024

f32 = jnp.float32
bf16 = jnp.bfloat16


def _sigmoid(z):
    return 1.0 / (1.0 + jnp.exp(-z))


def _in_proj_body(x_ref, ng_ref, w_ref, cq_ref, sq_ref, ck_ref, sk_ref, ones_ref,
                  u_ref, a_ref, q0_ref, q1_ref, q2_ref, k0_ref, k1_ref, k2_ref,
                  v0_ref, v1_ref, v2_ref, sza_ref, sgc_ref, sga_ref, xn_ref, perm_ref, t_ref):
    x = x_ref[...]
    ms = jnp.mean(x * x, axis=-1, keepdims=True)
    xn_ref[...] = (x * lax.rsqrt(ms + NORM_EPS) * ng_ref[...]).astype(bf16)

    def proj(c0):
        return jnp.dot(xn_ref[...], w_ref[:, c0:c0 + CHUNK], preferred_element_type=f32)

    n_perm_used = [0]

    def store_head_pair(group_refs, hp, val):
        g, h = divmod(hp, N_HP_GROUP)
        d = DILATIONS[g]
        if d == 1:
            group_refs[g][h] = val.astype(bf16)
            return
        buf = perm_ref.at[n_perm_used[0] % N_PERM]
        n_perm_used[0] += 1
        buf[...] = val
        for r in range(d):
            group_refs[g][h, :, r * HP_WIDTH:(r + 1) * HP_WIDTH] = (
                buf[pl.ds(r, TM_IN // d, stride=d), :].astype(bf16))

    for j in range(0, W_CONV, CHUNK):
        u_ref[:, j:j + CHUNK] = (proj(OFF_C + j) * proj(OFF_H + j)).astype(bf16)
        z = proj(OFF_ZC + j)
        a_ref[:, j:j + CHUNK] = (proj(OFF_B + j) * z * _sigmoid(z)).astype(bf16)

    lane = lax.broadcasted_iota(jnp.int32, (1, HP_WIDTH), 1)
    first_half = (lane % HEAD_DIM) < (HEAD_DIM // 2)

    def qk_epilogue(t, cos_ref, sin_ref, refs, c):
        ss = jnp.dot((t * t).astype(bf16), ones_ref[...], preferred_element_type=f32)
        r = lax.rsqrt(ss * (1.0 / HEAD_DIM) + NORM_EPS)
        for p in range(HP_PER_CHUNK):
            cols = slice(p * HP_WIDTH, (p + 1) * HP_WIDTH)
            th = t[:, cols]
            partner = jnp.where(first_half,
                                pltpu.roll(th, HP_WIDTH - HEAD_DIM // 2, axis=1),
                                pltpu.roll(th, HEAD_DIM // 2, axis=1))
            store_head_pair(refs, c * HP_PER_CHUNK + p,
                            (th * cos_ref[...] + partner * sin_ref[...]) * r[:, cols])

    qk_chunks = [(OFF_Q + c * CHUNK, cq_ref, sq_ref, (q0_ref, q1_ref, q2_ref), c)
                 for c in range(W_QKV // CHUNK)]
    qk_chunks += [(OFF_K + c * CHUNK, ck_ref, sk_ref, (k0_ref, k1_ref, k2_ref), c)
                  for c in range(W_QKV // CHUNK)]
    for i, ch in enumerate(qk_chunks):
        t_ref[i % 2] = proj(ch[0])
        if i > 0:
            qk_epilogue(t_ref[(i - 1) % 2], *qk_chunks[i - 1][1:])
    qk_epilogue(t_ref[(len(qk_chunks) - 1) % 2], *qk_chunks[-1][1:])

    for c in range(W_QKV // CHUNK):
        t = proj(OFF_V + c * CHUNK)
        for p in range(HP_PER_CHUNK):
            store_head_pair((v0_ref, v1_ref, v2_ref), c * HP_PER_CHUNK + p,
                            t[:, p * HP_WIDTH:(p + 1) * HP_WIDTH])

    for j in range(0, W_ATTN_OUT, CHUNK):
        z = proj(OFF_ZA + j)
        sza_ref[:, j:j + CHUNK] = (z * _sigmoid(z)).astype(bf16)
    for j in range(0, D_MODEL, CHUNK):
        sgc_ref[:, j:j + CHUNK] = _sigmoid(proj(OFF_GC + j)).astype(bf16)
        sga_ref[:, j:j + CHUNK] = _sigmoid(proj(OFF_GA + j)).astype(bf16)


def _in_proj(x, norm_g, w_in, tables, ones_bd):
    B, S, _ = x.shape
    nt = S // TM_IN
    const = lambda b, t: (0, 0)
    tok = lambda b, t: (b, t, 0)
    sq = pl.Squeezed()
    tok_spec = lambda w: pl.BlockSpec((sq, TM_IN, w), tok)
    tok_shape = lambda w: jax.ShapeDtypeStruct((B, S, w), bf16)
    grp_shapes = tuple(jax.ShapeDtypeStruct((B, N_HP_GROUP, S // d, d * HP_WIDTH), bf16)
                       for d in DILATIONS)
    grp_specs = tuple(pl.BlockSpec((sq, N_HP_GROUP, TM_IN // d, d * HP_WIDTH),
                                   lambda b, t: (b, 0, t, 0)) for d in DILATIONS)
    return pl.pallas_call(
        _in_proj_body,
        grid=(B, nt),
        in_specs=[
            tok_spec(D_MODEL),
            pl.BlockSpec((1, D_MODEL), const),
            pl.BlockSpec((D_MODEL, IN_WIDTH), const, pipeline_mode=pl.Buffered(1)),
        ] + [pl.BlockSpec((TM_IN, HP_WIDTH), lambda b, t: (t, 0)) for _ in range(4)] + [
            pl.BlockSpec((CHUNK, CHUNK), const),
        ],
        out_specs=(tok_spec(W_CONV), tok_spec(W_CONV)) + grp_specs * 3
                  + (tok_spec(W_ATTN_OUT), tok_spec(D_MODEL), tok_spec(D_MODEL)),
        out_shape=(tok_shape(W_CONV), tok_shape(W_CONV)) + grp_shapes * 3
                  + (tok_shape(W_ATTN_OUT), tok_shape(D_MODEL), tok_shape(D_MODEL)),
        scratch_shapes=[pltpu.VMEM((TM_IN, D_MODEL), bf16),
                        pltpu.VMEM((N_PERM, TM_IN, HP_WIDTH), f32),
                        pltpu.VMEM((2, TM_IN, CHUNK), f32)],
        compiler_params=pltpu.CompilerParams(
            dimension_semantics=("arbitrary", "arbitrary"),
            vmem_limit_bytes=VMEM_LIMIT_IN),
        name="in_proj",
    )(x, norm_g, w_in, *tables, ones_bd)


def _scores(q, ks):
    lane = lax.broadcasted_iota(jnp.int32, (BQ, HP_WIDTH), 1)
    head0 = lane < HEAD_DIM
    zero = jnp.zeros_like(q)
    q2 = jnp.concatenate([jnp.where(head0, q, zero), jnp.where(head0, zero, q)], axis=0)
    return lax.dot_general(q2, ks, (((1,), (1,)), ((), ())), preferred_element_type=f32)


def _softmax_pv(s, vs, bias):
    lane = lax.broadcasted_iota(jnp.int32, (BQ, HP_WIDTH), 1)
    head0 = lane < HEAD_DIM
    s = s + bias
    m = jnp.max(s, axis=1, keepdims=True)
    p = jnp.exp2(s - m).astype(bf16)
    vs1 = jnp.concatenate([vs, jnp.ones_like(vs)], axis=1)
    pv = jnp.dot(p, vs1, preferred_element_type=f32)
    acc = jnp.where(head0, pv[:BQ, :HP_WIDTH], pv[BQ:, :HP_WIDTH])
    l_sel = jnp.where(head0, pv[:BQ, HP_WIDTH:], pv[BQ:, HP_WIDTH:])
    m_sel = jnp.where(head0, m[:BQ], m[BQ:])
    return acc * (1.0 / l_sel), m_sel + jnp.log2(l_sel)


def _band_attn_body(bias_ref, q0_ref, k0_ref, v0_ref, q1_ref, k1_ref, v1_ref,
                    q2_ref, k2_ref, v2_ref, o_ref, o0_s, l0_s, o1_s, l1_s, o2_s, l2_s, s_scr):
    def addr(n, L):
        if isinstance(n, int):
            qs = n * BQ
            st = min(max(qs - BAND_HALF, 0), L - SPAN)
            return qs, st, (qs - st) // BAND_HALF
        qs = pl.multiple_of(n * BQ, BQ)
        st = pl.multiple_of(jnp.clip(n * BQ - BAND_HALF, 0, L - SPAN), BAND_HALF)
        return qs, st, lax.shift_right_logical(qs - st, BAND_HALF.bit_length() - 1)

    def scores_stage(blocks, slot):
        for j, (q_ref, k_ref, _, cols, n, L, _) in enumerate(blocks):
            qs, st, _ = addr(n, L)
            s_scr[slot, j] = _scores(q_ref[pl.ds(qs, BQ), cols], k_ref[pl.ds(st, SPAN), cols])

    def softmax_stage(blocks, slot, store):
        for j, (_, _, v_ref, cols, n, L, r) in enumerate(blocks):
            qs, st, case = addr(n, L)
            o, lse = _softmax_pv(s_scr[slot, j], v_ref[pl.ds(st, SPAN), cols], bias_ref[case])
            store(qs, o, lse, r)

    def run_group(trips, blocks_of, store, next_blocks):
        assert trips % STEP_ITERS == 0 and STEP_ITERS % 2 == 0

        def iters(i, last):
            for k in range(STEP_ITERS):
                nxt = blocks_of(i + k + 1) if not (last and k == STEP_ITERS - 1) else next_blocks
                if nxt is not None:
                    scores_stage(nxt, (k + 1) % 2)
                softmax_stage(blocks_of(i + k), k % 2, store)

        def step(h, carry):
            iters(STEP_ITERS * h, False)
            return carry

        lax.fori_loop(0, trips // STEP_ITERS - 1, step, 0)
        iters(trips - STEP_ITERS, True)

    d_last = DILATIONS[-1]
    pitch = o_ref.shape[0] // d_last + RES_PAD

    def make_store(o_s, l_s, d):
        def store(qs, o, lse, r):
            if d == 1:
                rows = pl.ds(qs, BQ)
            elif d == d_last:
                rows = pl.ds(pl.multiple_of(r * pitch + qs, 8), BQ)
            else:
                rows = pl.ds(qs * d + r, BQ, stride=d)
            o_s[rows, :] = o
            l_s[rows, :] = lse
        return store

    def blocks_of_group(q_ref, k_ref, v_ref, d):
        L = q_ref.shape[0]
        nb = L // BQ
        if d == 1:
            return nb // PIPE_BLOCKS, lambda i: [
                (q_ref, k_ref, v_ref, slice(None), PIPE_BLOCKS * i + j, L, 0) for j in range(PIPE_BLOCKS)]
        if d <= PIPE_BLOCKS:
            return nb, lambda i: [
                (q_ref, k_ref, v_ref, slice(r * HP_WIDTH, (r + 1) * HP_WIDTH), i, L, r) for r in range(d)]
        per_iter = PIPE_BLOCKS // nb

        def blocks(i):
            out = []
            for jr in range(per_iter):
                r = i * per_iter + jr
                cols = pl.ds(pl.multiple_of(r * HP_WIDTH, HP_WIDTH), HP_WIDTH)
                out += [(q_ref, k_ref, v_ref, cols, n, L, r) for n in range(nb)]
            return out
        return d // per_iter, blocks

    groups = []
    for (q_ref, k_ref, v_ref), o_s, l_s, d in (((q0_ref, k0_ref, v0_ref), o0_s, l0_s, DILATIONS[0]),
                                               ((q1_ref, k1_ref, v1_ref), o1_s, l1_s, DILATIONS[1]),
                                               ((q2_ref, k2_ref, v2_ref), o2_s, l2_s, DILATIONS[2])):
        trips, blocks = blocks_of_group(q_ref, k_ref, v_ref, d)
        groups.append((trips, blocks, make_store(o_s, l_s, d)))
    scores_stage(groups[0][1](0), 0)
    for g, (trips, blocks, store) in enumerate(groups):
        run_group(trips, blocks, store, groups[g + 1][1](0) if g + 1 < len(groups) else None)

    def merge(c, carry):
        rows = pl.ds(pl.multiple_of(c * BQ, BQ), BQ)

        def last_group(ref):
            base = c * (BQ // d_last)
            return jnp.concatenate([ref[pl.ds(base + a, d_last, stride=pitch), :]
                                    for a in range(BQ // d_last)], axis=0)

        lse0, lse1, lse2 = l0_s[rows, :], l1_s[rows, :], last_group(l2_s)
        mx = jnp.maximum(jnp.maximum(lse0, lse1), lse2)
        w0, w1, w2 = jnp.exp2(lse0 - mx), jnp.exp2(lse1 - mx), jnp.exp2(lse2 - mx)
        o = (w0 * o0_s[rows, :] + w1 * o1_s[rows, :] + w2 * last_group(o2_s)) * (1.0 / (w0 + w1 + w2))
        o_ref[rows, :] = o.astype(bf16)
        return carry

    lax.fori_loop(0, o_ref.shape[0] // BQ, merge, 0, unroll=2)


def _band_bias():
    qi = np.arange(2 * BQ)[:, None] % BQ
    kj = np.arange(SPAN)[None, :]
    rel = np.stack([kj - qi - c * BAND_HALF for c in range(N_BIAS)])
    return jnp.asarray(np.where(np.abs(rel) <= BAND_HALF, 0.0, NEG_BIG), dtype=f32)


def _band_attn(qkv):
    B = qkv[0].shape[0]
    S = qkv[0].shape[2]
    sq = pl.Squeezed()
    hp_block = lambda b, h: (b, h, 0, 0)
    operands, specs = [], []
    for g, d in enumerate(DILATIONS):
        for a in range(3):
            operands.append(qkv[a * N_GROUPS + g])
            specs.append(pl.BlockSpec((sq, sq, S // d, d * HP_WIDTH), hp_block))
    return pl.pallas_call(
        _band_attn_body,
        grid=(B, N_HP_GROUP),
        in_specs=[pl.BlockSpec((N_BIAS, 2 * BQ, SPAN), lambda b, h: (0, 0, 0))] + specs,
        out_specs=pl.BlockSpec((sq, sq, S, HP_WIDTH), hp_block),
        out_shape=jax.ShapeDtypeStruct((B, N_HP_GROUP, S, HP_WIDTH), bf16),
        scratch_shapes=[pltpu.VMEM((S, HP_WIDTH), f32) for _ in range(2 * N_GROUPS - 2)]
                       + [pltpu.VMEM((S + DILATIONS[-1] * RES_PAD, HP_WIDTH), f32) for _ in range(2)]
                       + [pltpu.VMEM((2, PIPE_BLOCKS, 2 * BQ, SPAN), f32)],
        compiler_params=pltpu.CompilerParams(
            dimension_semantics=("arbitrary", "arbitrary"),
            vmem_limit_bytes=VMEM_LIMIT_ATTN),
        name="band_attn",
    )(_band_bias(), *operands)


def _out_proj_body(x_ref, u_ref, up_ref, un_ref, a_ref, o_ref, sza_ref, sgc_ref, sga_ref,
                   cw_ref, cb_ref, wbc_ref, wba_ref, wo_ref, out_ref):
    t = pl.program_id(1)
    nt = pl.num_programs(1)
    u = u_ref[...].astype(f32)
    prev_row = up_ref[...].astype(f32)[BF16_ROWS - 1:BF16_ROWS] * jnp.where(t > 0, 1.0, 0.0)
    next_row = un_ref[...].astype(f32)[0:1] * jnp.where(t < nt - 1, 1.0, 0.0)
    row = lax.broadcasted_iota(jnp.int32, (F32_ROWS, 1), 0)
    rolled = pltpu.roll(u, 1, axis=0)
    u_prev = jnp.concatenate([jnp.where(row == 0, prev_row, rolled[:F32_ROWS]), rolled[F32_ROWS:]], axis=0)
    rolled = pltpu.roll(u, TM_OUT - 1, axis=0)
    u_next = jnp.concatenate([rolled[:-F32_ROWS],
                              jnp.where(row == F32_ROWS - 1, next_row, rolled[-F32_ROWS:])], axis=0)
    o_cat = jnp.concatenate([o_ref[h] for h in range(N_HP_GROUP)], axis=1)
    p_a = jnp.dot(o_cat * sza_ref[...], wba_ref[...], preferred_element_type=f32)
    p_c = None
    for j in range(0, W_CONV, CHUNK):
        cols = slice(j, j + CHUNK)
        conv = (u_prev[:, cols] * cw_ref[0:1, cols] + u[:, cols] * cw_ref[1:2, cols]
                + u_next[:, cols] * cw_ref[2:3, cols] + cb_ref[:, cols])
        y_c = (a_ref[:, cols].astype(f32) * conv).astype(bf16)
        part = jnp.dot(y_c, wbc_ref[cols, :], preferred_element_type=f32)
        p_c = part if p_c is None else p_c + part
    m =(sgc_ref[...].astype(f32) * p_c + sga_ref[...].astype(f32) * p_a).astype(bf16)
    out_ref[...] = x_ref[...] + jnp.dot(m, wo_ref[...], preferred_element_type=f32)


def _out_proj(x, u, a, o, sza, sgc, sga, conv_w, conv_b, wbc, wba, wo):
    B, S, _ = x.shape
    nt = S // TM_OUT
    rb = TM_OUT // BF16_ROWS
    sq = pl.Squeezed()
    const = lambda b, t: (0, 0)
    tok = lambda b, t: (b, t, 0)
    tok_spec = lambda w: pl.BlockSpec((sq, TM_OUT, w), tok)
    halo_prev = pl.BlockSpec((sq, BF16_ROWS, W_CONV), lambda b, t: (b, jnp.maximum(t * rb - 1, 0), 0))
    halo_next = pl.BlockSpec((sq, BF16_ROWS, W_CONV),
                             lambda b, t: (b, jnp.minimum((t + 1) * rb, S // BF16_ROWS - 1), 0))
    return pl.pallas_call(
        _out_proj_body,
        grid=(B, nt),
        in_specs=[
            tok_spec(D_MODEL), tok_spec(W_CONV), halo_prev, halo_next, tok_spec(W_CONV),
            pl.BlockSpec((sq, N_HP_GROUP, TM_OUT, HP_WIDTH), lambda b, t: (b, 0, t, 0)),
            tok_spec(W_ATTN_OUT), tok_spec(D_MODEL), tok_spec(D_MODEL),
            pl.BlockSpec((3, W_CONV), const), pl.BlockSpec((1, W_CONV), const),
            pl.BlockSpec((W_CONV, D_MODEL), const), pl.BlockSpec((W_ATTN_OUT, D_MODEL), const),
            pl.BlockSpec((D_MODEL, D_MODEL), const),
        ],
        out_specs=tok_spec(D_MODEL),
        out_shape=jax.ShapeDtypeStruct((B, S, D_MODEL), f32),
        compiler_params=pltpu.CompilerParams(
            dimension_semantics=("arbitrary", "arbitrary"),
            vmem_limit_bytes=VMEM_LIMIT_OUT),
        name="out_proj",
    )(x, u, u, u, a, o, sza, sgc, sga, conv_w, conv_b, wbc, wba, wo)


def _rotary_tables(S, q_gain, k_gain):
    half = HEAD_DIM // 2
    inv_freq = ROPE_THETA ** (-jnp.arange(0, half, dtype=f32) / half)
    ang = jnp.arange(S, dtype=f32)[:, None] * inv_freq[None, :]
    cos, sin = jnp.cos(ang), jnp.sin(ang)
    reps = HP_WIDTH // HEAD_DIM
    cos_t = jnp.tile(jnp.concatenate([cos, cos], axis=1), (1, reps))
    sin_t = jnp.tile(jnp.concatenate([-sin, sin], axis=1), (1, reps))
    tables = []
    for g, scale in ((q_gain, LOG2E / math.sqrt(HEAD_DIM)), (k_gain, 1.0)):
        g = g.astype(f32) * scale
        g_swap = jnp.concatenate([g[half:], g[:half]])
        tables += [cos_t * jnp.tile(g, reps)[None, :], sin_t * jnp.tile(g_swap, reps)[None, :]]
    return tables


def kernel(x, norm_g, w_in, conv_w, conv_b, q_norm_g, k_norm_g, w_branch_conv, w_branch_attn, w_out):
    B, S, D = x.shape
    assert D == D_MODEL and w_in.shape == (D_MODEL, IN_WIDTH)
    assert S % TM_IN == 0 and S % TM_OUT == 0 and (S // DILATIONS[-1]) % BQ == 0
    assert (TM_IN // DILATIONS[-1]) % BF16_ROWS == 0 and (S // BQ) % PIPE_BLOCKS == 0
    head_id = np.arange(CHUNK) // HEAD_DIM
    ones_bd = jnp.asarray(head_id[:, None] == head_id[None, :], dtype=bf16)

    outs = _in_proj(x, norm_g.astype(f32)[None, :], w_in.astype(bf16),
                    _rotary_tables(S, q_norm_g, k_norm_g), ones_bd)
    u, a = outs[0], outs[1]
    sza, sgc, sga = outs[11], outs[12], outs[13]
    o = _band_attn(outs[2:11])
    return _out_proj(x, u, a, o, sza, sgc, sga, conv_w.astype(f32), conv_b.astype(f32)[None, :],
                     w_branch_conv.astype(bf16), w_branch_attn.astype(bf16), w_out.astype(bf16))
```

```python
import math

import jax
import jax.numpy as jnp
import numpy as np
from jax import lax
from jax.experimental import pallas as pl
from jax.experimental.pallas import tpu as pltpu

D_MODEL = 1024
W_CONV = 1024
HEAD_DIM = 64
HEADS_PER_GROUP = 8
DILATIONS = (1, 4, 16)
BAND_HALF = 64
N_GROUPS = 3
W_QKV = N_GROUPS * HEADS_PER_GROUP * HEAD_DIM
W_ATTN_OUT = HEADS_PER_GROUP * HEAD_DIM
ROPE_THETA = 10000.0
NORM_EPS = 1e-6

OFF_B, OFF_C, OFF_H, OFF_ZC = 0, 1024, 2048, 3072
OFF_Q = 4096
OFF_K = OFF_Q + W_QKV
OFF_V = OFF_K + W_QKV
OFF_ZA = OFF_V + W_QKV
OFF_GC = OFF_ZA + W_ATTN_OUT
OFF_GA = OFF_GC + D_MODEL
IN_WIDTH = OFF_GA + D_MODEL

LANES = 128
MXU_WIDTH = 256
HP_WIDTH = 2 * HEAD_DIM
N_HP_GROUP = HEADS_PER_GROUP // 2
F32_ROWS = 8
BF16_ROWS = 16

TM_IN = 512
TM_OUT = 512
CHUNK = MXU_WIDTH
HP_PER_CHUNK = CHUNK // HP_WIDTH
PERM_FREE_STRIDE = 4
N_PERM = 4
BQ = 128
SPAN = BQ + 2 * BAND_HALF
RES_PAD = 8
STEP_ITERS = 4
PIPE_BLOCKS = 4
N_BIAS = SPAN // BAND_HALF - 1
NEG_BIG = -1e30
LOG2E = math.log2(math.e)

VMEM_LIMIT_IN = 58 * 1024 * 1024
VMEM_LIMIT_ATTN = 48 * 1024 * 1024
VMEM_LIMIT_OUT = 48 * 1024 * 1024

f32 = jnp.float32
bf16 = jnp.bfloat16


def _perm_pitch(d):
    return d if d <= PERM_FREE_STRIDE else d + F32_ROWS


PERM_ROWS = max(TM_IN // d * _perm_pitch(d) for d in DILATIONS)


def _sigmoid(z):
    return 1.0 / (1.0 + jnp.exp(-z))


def _in_proj_body(x_ref, ng_ref, w_ref, cq_ref, sq_ref, ck_ref, sk_ref, ones_ref,
                  u_ref, a_ref, q0_ref, q1_ref, q2_ref, k0_ref, k1_ref, k2_ref,
                  v0_ref, v1_ref, v2_ref, sza_ref, sgc_ref, sga_ref, xn_ref, perm_ref, t_ref):
    x = x_ref[...]
    ms = jnp.mean(x * x, axis=-1, keepdims=True)
    xn_ref[...] = (x * lax.rsqrt(ms + NORM_EPS) * ng_ref[...]).astype(bf16)

    def proj(c0):
        return jnp.dot(xn_ref[...], w_ref[:, c0:c0 + CHUNK], preferred_element_type=f32)

    n_perm_used = [0]

    def store_head_pair(group_refs, hp, val):
        g, h = divmod(hp, N_HP_GROUP)
        d = DILATIONS[g]
        if d == 1:
            group_refs[g][h] = val.astype(bf16)
            return
        buf = perm_ref.at[n_perm_used[0] % N_PERM]
        n_perm_used[0] += 1
        pitch = _perm_pitch(d)
        if pitch == d:
            buf[0:TM_IN, :] = val
        else:
            for j in range(TM_IN // d):
                buf[j * pitch:j * pitch + d, :] = val[j * d:(j + 1) * d, :]
        for r in range(d):
            group_refs[g][h, :, r * HP_WIDTH:(r + 1) * HP_WIDTH] = (
                buf[pl.ds(r, TM_IN // d, stride=pitch), :].astype(bf16))

    for j in range(0, W_CONV, CHUNK):
        u_ref[:, j:j + CHUNK] = (proj(OFF_C + j) * proj(OFF_H + j)).astype(bf16)
        z = proj(OFF_ZC + j)
        a_ref[:, j:j + CHUNK] = (proj(OFF_B + j) * z * _sigmoid(z)).astype(bf16)

    lane = lax.broadcasted_iota(jnp.int32, (1, HP_WIDTH), 1)
    first_half = (lane % HEAD_DIM) < (HEAD_DIM // 2)

    def qk_epilogue(t, cos_ref, sin_ref, refs, c):
        ss = jnp.dot((t * t).astype(bf16), ones_ref[...], preferred_element_type=f32)
        r = lax.rsqrt(ss * (1.0 / HEAD_DIM) + NORM_EPS)
        for p in range(HP_PER_CHUNK):
            cols = slice(p * HP_WIDTH, (p + 1) * HP_WIDTH)
            th = t[:, cols]
            partner = jnp.where(first_half,
                                pltpu.roll(th, HP_WIDTH - HEAD_DIM // 2, axis=1),
                                pltpu.roll(th, HEAD_DIM // 2, axis=1))
            store_head_pair(refs, c * HP_PER_CHUNK + p,
                            (th * cos_ref[...] + partner * sin_ref[...]) * r[:, cols])

    qk_chunks = [(OFF_Q + c * CHUNK, cq_ref, sq_ref, (q0_ref, q1_ref, q2_ref), c)
                 for c in range(W_QKV // CHUNK)]
    qk_chunks += [(OFF_K + c * CHUNK, ck_ref, sk_ref, (k0_ref, k1_ref, k2_ref), c)
                  for c in range(W_QKV // CHUNK)]
    for i, ch in enumerate(qk_chunks):
        t_ref[i % 2] = proj(ch[0])
        if i > 0:
            qk_epilogue(t_ref[(i - 1) % 2], *qk_chunks[i - 1][1:])
    qk_epilogue(t_ref[(len(qk_chunks) - 1) % 2], *qk_chunks[-1][1:])

    for c in range(W_QKV // CHUNK):
        t = proj(OFF_V + c * CHUNK)
        for p in range(HP_PER_CHUNK):
            store_head_pair((v0_ref, v1_ref, v2_ref), c * HP_PER_CHUNK + p,
                            t[:, p * HP_WIDTH:(p + 1) * HP_WIDTH])

    for j in range(0, W_ATTN_OUT, CHUNK):
        z = proj(OFF_ZA + j)
        sza_ref[:, j:j + CHUNK] = (z * _sigmoid(z)).astype(bf16)
    for j in range(0, D_MODEL, CHUNK):
        sgc_ref[:, j:j + CHUNK] = _sigmoid(proj(OFF_GC + j)).astype(bf16)
        sga_ref[:, j:j + CHUNK] = _sigmoid(proj(OFF_GA + j)).astype(bf16)


def _in_proj(x, norm_g, w_in, tables, ones_bd):
    B, S, _ = x.shape
    nt = S // TM_IN
    const = lambda b, t: (0, 0)
    tok = lambda b, t: (b, t, 0)
    sq = pl.Squeezed()
    tok_spec = lambda w: pl.BlockSpec((sq, TM_IN, w), tok)
    tok_shape = lambda w: jax.ShapeDtypeStruct((B, S, w), bf16)
    grp_shapes = tuple(jax.ShapeDtypeStruct((B, N_HP_GROUP, S // d, d * HP_WIDTH), bf16)
                       for d in DILATIONS)
    grp_specs = tuple(pl.BlockSpec((sq, N_HP_GROUP, TM_IN // d, d * HP_WIDTH),
                                   lambda b, t: (b, 0, t, 0)) for d in DILATIONS)
    return pl.pallas_call(
        _in_proj_body,
        grid=(B, nt),
        in_specs=[
            tok_spec(D_MODEL),
            pl.BlockSpec((1, D_MODEL), const),
            pl.BlockSpec((D_MODEL, IN_WIDTH), const, pipeline_mode=pl.Buffered(1)),
        ] + [pl.BlockSpec((TM_IN, HP_WIDTH), lambda b, t: (t, 0)) for _ in range(4)] + [
            pl.BlockSpec((CHUNK, CHUNK), const),
        ],
        out_specs=(tok_spec(W_CONV), tok_spec(W_CONV)) + grp_specs * 3
                  + (tok_spec(W_ATTN_OUT), tok_spec(D_MODEL), tok_spec(D_MODEL)),
        out_shape=(tok_shape(W_CONV), tok_shape(W_CONV)) + grp_shapes * 3
                  + (tok_shape(W_ATTN_OUT), tok_shape(D_MODEL), tok_shape(D_MODEL)),
        scratch_shapes=[pltpu.VMEM((TM_IN, D_MODEL), bf16),
                        pltpu.VMEM((N_PERM, PERM_ROWS, HP_WIDTH), f32),
                        pltpu.VMEM((2, TM_IN, CHUNK), f32)],
        compiler_params=pltpu.CompilerParams(
            dimension_semantics=("arbitrary", "arbitrary"),
            vmem_limit_bytes=VMEM_LIMIT_IN),
        name="in_proj",
    )(x, norm_g, w_in, *tables, ones_bd)


def _scores(q, ks):
    lane = lax.broadcasted_iota(jnp.int32, (BQ, HP_WIDTH), 1)
    head0 = lane < HEAD_DIM
    zero = jnp.zeros_like(q)
    q2 = jnp.concatenate([jnp.where(head0, q, zero), jnp.where(head0, zero, q)], axis=0)
    return lax.dot_general(q2, ks, (((1,), (1,)), ((), ())), preferred_element_type=f32)


def _softmax_pv(s, vs, bias):
    lane = lax.broadcasted_iota(jnp.int32, (BQ, HP_WIDTH), 1)
    head0 = lane < HEAD_DIM
    s = s + bias
    m = jnp.max(s, axis=1, keepdims=True)
    p = jnp.exp2(s - m).astype(bf16)
    vs1 = jnp.concatenate([vs, jnp.ones_like(vs)], axis=1)
    pv = jnp.dot(p, vs1, preferred_element_type=f32)
    acc = jnp.where(head0, pv[:BQ, :HP_WIDTH], pv[BQ:, :HP_WIDTH])
    l_sel = jnp.where(head0, pv[:BQ, HP_WIDTH:], pv[BQ:, HP_WIDTH:])
    m_sel = jnp.where(head0, m[:BQ], m[BQ:])
    return acc * (1.0 / l_sel), m_sel + jnp.log2(l_sel)


def _band_attn_body(bias_ref, q0_ref, k0_ref, v0_ref, q1_ref, k1_ref, v1_ref,
                    q2_ref, k2_ref, v2_ref, o_ref, o0_s, l0_s, o1_s, l1_s, o2_s, l2_s, s_scr):
    def addr(n, L):
        if isinstance(n, int):
            qs = n * BQ
            st = min(max(qs - BAND_HALF, 0), L - SPAN)
            return qs, st, (qs - st) // BAND_HALF
        qs = pl.multiple_of(n * BQ, BQ)
        st = pl.multiple_of(jnp.clip(n * BQ - BAND_HALF, 0, L - SPAN), BAND_HALF)
        return qs, st, lax.shift_right_logical(qs - st, BAND_HALF.bit_length() - 1)

    def scores_stage(blocks, slot):
        for j, (q_ref, k_ref, _, cols, n, L, _) in enumerate(blocks):
            qs, st, _ = addr(n, L)
            s_scr[slot, j] = _scores(q_ref[pl.ds(qs, BQ), cols], k_ref[pl.ds(st, SPAN), cols])

    def softmax_stage(blocks, slot, store):
        for j, (_, _, v_ref, cols, n, L, r) in enumerate(blocks):
            qs, st, case = addr(n, L)
            o, lse = _softmax_pv(s_scr[slot, j], v_ref[pl.ds(st, SPAN), cols], bias_ref[case])
            store(qs, o, lse, r)

    def run_group(trips, blocks_of, store, next_blocks):
        assert trips % STEP_ITERS == 0 and STEP_ITERS % 2 == 0

        def iters(i, last):
            for k in range(STEP_ITERS):
                nxt = blocks_of(i + k + 1) if not (last and k == STEP_ITERS - 1) else next_blocks
                if nxt is not None:
                    scores_stage(nxt, (k + 1) % 2)
                softmax_stage(blocks_of(i + k), k % 2, store)

        def step(h, carry):
            iters(STEP_ITERS * h, False)
            return carry

        lax.fori_loop(0, trips // STEP_ITERS - 1, step, 0)
        iters(trips - STEP_ITERS, True)

    d_last = DILATIONS[-1]
    pitch = o_ref.shape[0] // d_last + RES_PAD

    def make_store(o_s, l_s, d):
        def store(qs, o, lse, r):
            if d == 1:
                rows = pl.ds(qs, BQ)
            elif d == d_last:
                rows = pl.ds(pl.multiple_of(r * pitch + qs, 8), BQ)
            else:
                rows = pl.ds(qs * d + r, BQ, stride=d)
            o_s[rows, :] = o
            l_s[rows, :] = lse
        return store

    def blocks_of_group(q_ref, k_ref, v_ref, d):
        L = q_ref.shape[0]
        nb = L // BQ
        if d == 1:
            return nb // PIPE_BLOCKS, lambda i: [
                (q_ref, k_ref, v_ref, slice(None), PIPE_BLOCKS * i + j, L, 0) for j in range(PIPE_BLOCKS)]
        if d <= PIPE_BLOCKS:
            return nb, lambda i: [
                (q_ref, k_ref, v_ref, slice(r * HP_WIDTH, (r + 1) * HP_WIDTH), i, L, r) for r in range(d)]
        per_iter = PIPE_BLOCKS // nb

        def blocks(i):
            out = []
            for jr in range(per_iter):
                r = i * per_iter + jr
                cols = pl.ds(pl.multiple_of(r * HP_WIDTH, HP_WIDTH), HP_WIDTH)
                out += [(q_ref, k_ref, v_ref, cols, n, L, r) for n in range(nb)]
            return out
        return d // per_iter, blocks

    groups = []
    for (q_ref, k_ref, v_ref), o_s, l_s, d in (((q0_ref, k0_ref, v0_ref), o0_s, l0_s, DILATIONS[0]),
                                               ((q1_ref, k1_ref, v1_ref), o1_s, l1_s, DILATIONS[1]),
                                               ((q2_ref, k2_ref, v2_ref), o2_s, l2_s, DILATIONS[2])):
        trips, blocks = blocks_of_group(q_ref, k_ref, v_ref, d)
        groups.append((trips, blocks, make_store(o_s, l_s, d)))
    scores_stage(groups[0][1](0), 0)
    for g, (trips, blocks, store) in enumerate(groups):
        run_group(trips, blocks, store, groups[g + 1][1](0) if g + 1 < len(groups) else None)

    def merge(c, carry):
        rows = pl.ds(pl.multiple_of(c * BQ, BQ), BQ)

        def last_group(ref):
            base = c * (BQ // d_last)
            return jnp.concatenate([ref[pl.ds(base + a, d_last, stride=pitch), :]
                                    for a in range(BQ // d_last)], axis=0)

        lse0, lse1, lse2 = l0_s[rows, :], l1_s[rows, :], last_group(l2_s)
        mx = jnp.maximum(jnp.maximum(lse0, lse1), lse2)
        w0, w1, w2 = jnp.exp2(lse0 - mx), jnp.exp2(lse1 - mx), jnp.exp2(lse2 - mx)
        o = (w0 * o0_s[rows, :] + w1 * o1_s[rows, :] + w2 * last_group(o2_s)) * (1.0 / (w0 + w1 + w2))
        o_ref[rows, :] = o.astype(bf16)
        return carry

    lax.fori_loop(0, o_ref.shape[0] // BQ, merge, 0, unroll=4)


def _band_bias():
    qi = np.arange(2 * BQ)[:, None] % BQ
    kj = np.arange(SPAN)[None, :]
    rel = np.stack([kj - qi - c * BAND_HALF for c in range(N_BIAS)])
    return jnp.asarray(np.where(np.abs(rel) <= BAND_HALF, 0.0, NEG_BIG), dtype=f32)


def _band_attn(qkv):
    B = qkv[0].shape[0]
    S = qkv[0].shape[2]
    sq = pl.Squeezed()
    hp_block = lambda b, h: (b, h, 0, 0)
    operands, specs = [], []
    for g, d in enumerate(DILATIONS):
        for a in range(3):
            operands.append(qkv[a * N_GROUPS + g])
            specs.append(pl.BlockSpec((sq, sq, S // d, d * HP_WIDTH), hp_block))
    return pl.pallas_call(
        _band_attn_body,
        grid=(B, N_HP_GROUP),
        in_specs=[pl.BlockSpec((N_BIAS, 2 * BQ, SPAN), lambda b, h: (0, 0, 0))] + specs,
        out_specs=pl.BlockSpec((sq, sq, S, HP_WIDTH), hp_block),
        out_shape=jax.ShapeDtypeStruct((B, N_HP_GROUP, S, HP_WIDTH), bf16),
        scratch_shapes=[pltpu.VMEM((S, HP_WIDTH), f32) for _ in range(2 * N_GROUPS - 2)]
                       + [pltpu.VMEM((S + DILATIONS[-1] * RES_PAD, HP_WIDTH), f32) for _ in range(2)]
                       + [pltpu.VMEM((2, PIPE_BLOCKS, 2 * BQ, SPAN), f32)],
        compiler_params=pltpu.CompilerParams(
            dimension_semantics=("arbitrary", "arbitrary"),
            vmem_limit_bytes=VMEM_LIMIT_ATTN),
        name="band_attn",
    )(_band_bias(), *operands)


def _out_proj_body(x_ref, u_ref, up_ref, un_ref, a_ref, o_ref, sza_ref, sgc_ref, sga_ref,
                   cw_ref, cb_ref, wbc_ref, wba_ref, wo_ref, out_ref):
    t = pl.program_id(1)
    nt = pl.num_programs(1)
    u = u_ref[...].astype(f32)
    prev_row = up_ref[...].astype(f32)[BF16_ROWS - 1:BF16_ROWS] * jnp.where(t > 0, 1.0, 0.0)
    next_row = un_ref[...].astype(f32)[0:1] * jnp.where(t < nt - 1, 1.0, 0.0)
    row = lax.broadcasted_iota(jnp.int32, (F32_ROWS, 1), 0)
    rolled = pltpu.roll(u, 1, axis=0)
    u_prev = jnp.concatenate([jnp.where(row == 0, prev_row, rolled[:F32_ROWS]), rolled[F32_ROWS:]], axis=0)
    rolled = pltpu.roll(u, TM_OUT - 1, axis=0)
    u_next = jnp.concatenate([rolled[:-F32_ROWS],
                              jnp.where(row == F32_ROWS - 1, next_row, rolled[-F32_ROWS:])], axis=0)
    o_cat = jnp.concatenate([o_ref[h] for h in range(N_HP_GROUP)], axis=1)
    p_a = jnp.dot(o_cat * sza_ref[...], wba_ref[...], preferred_element_type=f32)
    p_c = None
    for j in range(0, W_CONV, CHUNK):
        cols = slice(j, j + CHUNK)
        conv = (u_prev[:, cols] * cw_ref[0:1, cols] + u[:, cols] * cw_ref[1:2, cols]
                + u_next[:, cols] * cw_ref[2:3, cols] + cb_ref[:, cols])
        y_c = (a_ref[:, cols].astype(f32) * conv).astype(bf16)
        part = jnp.dot(y_c, wbc_ref[cols, :], preferred_element_type=f32)
        p_c = part if p_c is None else p_c + part
    m = (sgc_ref[...].astype(f32) * p_c + sga_ref[...].astype(f32) * p_a).astype(bf16)
    out_ref[...] = x_ref[...] + jnp.dot(m, wo_ref[...], preferred_element_type=f32)


def _out_proj(x, u, a, o, sza, sgc, sga, conv_w, conv_b, wbc, wba, wo):
    B, S, _ = x.shape
    nt = S // TM_OUT
    rb = TM_OUT // BF16_ROWS
    sq = pl.Squeezed()
    const = lambda b, t: (0, 0)
    tok = lambda b, t: (b, t, 0)
    tok_spec = lambda w: pl.BlockSpec((sq, TM_OUT, w), tok)
    halo_prev = pl.BlockSpec((sq, BF16_ROWS, W_CONV), lambda b, t: (b, jnp.maximum(t * rb - 1, 0), 0))
    halo_next = pl.BlockSpec((sq, BF16_ROWS, W_CONV),
                             lambda b, t: (b, jnp.minimum((t + 1) * rb, S // BF16_ROWS - 1), 0))
    return pl.pallas_call(
        _out_proj_body,
        grid=(B, nt),
        in_specs=[
            tok_spec(D_MODEL), tok_spec(W_CONV), halo_prev, halo_next, tok_spec(W_CONV),
            pl.BlockSpec((sq, N_HP_GROUP, TM_OUT, HP_WIDTH), lambda b, t: (b, 0, t, 0)),
            tok_spec(W_ATTN_OUT), tok_spec(D_MODEL), tok_spec(D_MODEL),
            pl.BlockSpec((3, W_CONV), const), pl.BlockSpec((1, W_CONV), const),
            pl.BlockSpec((W_CONV, D_MODEL), const), pl.BlockSpec((W_ATTN_OUT, D_MODEL), const),
            pl.BlockSpec((D_MODEL, D_MODEL), const),
        ],
        out_specs=tok_spec(D_MODEL),
        out_shape=jax.ShapeDtypeStruct((B, S, D_MODEL), f32),
        compiler_params=pltpu.CompilerParams(
            dimension_semantics=("arbitrary", "arbitrary"),
            vmem_limit_bytes=VMEM_LIMIT_OUT),
        name="out_proj",
    )(x, u, u, u, a, o, sza, sgc, sga, conv_w, conv_b, wbc, wba, wo)


def _rotary_tables(S, q_gain, k_gain):
    half = HEAD_DIM // 2
    inv_freq = ROPE_THETA ** (-jnp.arange(0, half, dtype=f32) / half)
    ang = jnp.arange(S, dtype=f32)[:, None] * inv_freq[None, :]
    cos, sin = jnp.cos(ang), jnp.sin(ang)
    reps = HP_WIDTH // HEAD_DIM
    cos_t = jnp.tile(jnp.concatenate([cos, cos], axis=1), (1, reps))
    sin_t = jnp.tile(jnp.concatenate([-sin, sin], axis=1), (1, reps))
    tables = []
    for g, scale in ((q_gain, LOG2E / math.sqrt(HEAD_DIM)), (k_gain, 1.0)):
        g = g.astype(f32) * scale
        g_swap = jnp.concatenate([g[half:], g[:half]])
        tables += [cos_t * jnp.tile(g, reps)[None, :], sin_t * jnp.tile(g_swap, reps)[None, :]]
    return tables


def kernel(x, norm_g, w_in, conv_w, conv_b, q_norm_g, k_norm_g, w_branch_conv, w_branch_attn, w_out):
    B, S, D = x.shape
    assert D == D_MODEL and w_in.shape == (D_MODEL, IN_WIDTH)
    assert S % TM_IN == 0 and S % TM_OUT == 0 and (S // DILATIONS[-1]) % BQ == 0
    assert (TM_IN // DILATIONS[-1]) % BF16_ROWS == 0 and (S // BQ) % PIPE_BLOCKS == 0
    head_id = np.arange(CHUNK) // HEAD_DIM
    ones_bd = jnp.asarray(head_id[:, None] == head_id[None, :], dtype=bf16)

    outs = _in_proj(x, norm_g.astype(f32)[None, :], w_in.astype(bf16),
                    _rotary_tables(S, q_norm_g, k_norm_g), ones_bd)
    u, a = outs[0], outs[1]
    sza, sgc, sga = outs[11], outs[12], outs[13]
    o = _band_attn(outs[2:11])
    return _out_proj(x, u, a, o, sza, sgc, sga, conv_w.astype(f32), conv_b.astype(f32)[None, :],
                     w_branch_conv.astype(bf16), w_branch_attn.astype(bf16), w_out.astype(bf16))
```

```python
import math

import jax
import jax.numpy as jnp
import numpy as np
from jax import lax
from jax.experimental import pallas as pl
from jax.experimental.pallas import tpu as pltpu

D_MODEL = 1024
W_CONV = 1024
HEAD_DIM = 64
HEADS_PER_GROUP = 8
DILATIONS = (1, 4, 16)
BAND_HALF = 64
N_GROUPS = 3
W_QKV = N_GROUPS * HEADS_PER_GROUP * HEAD_DIM
W_ATTN_OUT = HEADS_PER_GROUP * HEAD_DIM
ROPE_THETA = 10000.0
NORM_EPS = 1e-6

OFF_B, OFF_C, OFF_H, OFF_ZC = 0, 1024, 2048, 3072
OFF_Q = 4096
OFF_K = OFF_Q + W_QKV
OFF_V = OFF_K + W_QKV
OFF_ZA = OFF_V + W_QKV
OFF_GC = OFF_ZA + W_ATTN_OUT
OFF_GA = OFF_GC + D_MODEL
IN_WIDTH = OFF_GA + D_MODEL

LANES = 128
MXU_WIDTH = 256
HP_WIDTH = 2 * HEAD_DIM
N_HP_GROUP = HEADS_PER_GROUP // 2
F32_ROWS = 8
BF16_ROWS = 16

TM_IN = 512
TM_OUT = 512
CHUNK = MXU_WIDTH
HP_PER_CHUNK = CHUNK // HP_WIDTH
PERM_FREE_STRIDE = 4
N_PERM = 4
BQ = 128
SPAN = BQ + 2 * BAND_HALF
RES_PAD = 8
STEP_ITERS = 4
PIPE_BLOCKS = 4
N_BIAS = SPAN // BAND_HALF - 1
NEG_BIG = -1e30
LOG2E = math.log2(math.e)

VMEM_LIMIT_IN = 58 * 1024 * 1024
VMEM_LIMIT_ATTN = 48 * 1024 * 1024
VMEM_LIMIT_OUT = 48 * 1024 * 1024

f32 = jnp.float32
bf16 = jnp.bfloat16


def _perm_pitch(d):
    return d if d <= PERM_FREE_STRIDE else d + F32_ROWS


PERM_ROWS = max(TM_IN // d * _perm_pitch(d) for d in DILATIONS)


def _sigmoid(z):
    return 1.0 / (1.0 + jnp.exp(-z))


def _in_proj_body(x_ref, ng_ref, w_ref, cq_ref, sq_ref, ck_ref, sk_ref, ones_ref,
                  u_ref, a_ref, q0_ref, q1_ref, q2_ref, k0_ref, k1_ref, k2_ref,
                  v0_ref, v1_ref, v2_ref, sza_ref, sgc_ref, sga_ref, xn_ref, perm_ref, t_ref):
    x = x_ref[...]
    ms = jnp.mean(x * x, axis=-1, keepdims=True)
    xn_ref[...] = (x * lax.rsqrt(ms + NORM_EPS) * ng_ref[...]).astype(bf16)

    def proj(c0):
        return jnp.dot(xn_ref[...], w_ref[:, c0:c0 + CHUNK], preferred_element_type=f32)

    n_perm_used = [0]

    def store_head_pair(group_refs, hp, val):
        g, h = divmod(hp, N_HP_GROUP)
        d = DILATIONS[g]
        if d == 1:
            group_refs[g][h] = val.astype(bf16)
            return
        buf = perm_ref.at[n_perm_used[0] % N_PERM]
        n_perm_used[0] += 1
        pitch = _perm_pitch(d)
        if pitch == d:
            buf[0:TM_IN, :] = val
        else:
            for j in range(TM_IN // d):
                buf[j * pitch:j * pitch + d, :] = val[j * d:(j + 1) * d, :]
        for r in range(d):
            group_refs[g][h, :, r * HP_WIDTH:(r + 1) * HP_WIDTH] = (
                buf[pl.ds(r, TM_IN // d, stride=pitch), :].astype(bf16))

    for j in range(0, W_CONV, CHUNK):
        u_ref[:, j:j + CHUNK] = (proj(OFF_C + j) * proj(OFF_H + j)).astype(bf16)
        z = proj(OFF_ZC + j)
        a_ref[:, j:j + CHUNK] = (proj(OFF_B + j) * z * _sigmoid(z)).astype(bf16)

    lane = lax.broadcasted_iota(jnp.int32, (1, HP_WIDTH), 1)
    first_half = (lane % HEAD_DIM) < (HEAD_DIM // 2)

    def qk_epilogue(t, cos_ref, sin_ref, refs, c):
        ss = jnp.dot((t * t).astype(bf16), ones_ref[...], preferred_element_type=f32)
        r = lax.rsqrt(ss * (1.0 / HEAD_DIM) + NORM_EPS)
        for p in range(HP_PER_CHUNK):
            cols = slice(p * HP_WIDTH, (p + 1) * HP_WIDTH)
            th = t[:, cols]
            partner = jnp.where(first_half,
                                pltpu.roll(th, HP_WIDTH - HEAD_DIM // 2, axis=1),
                                pltpu.roll(th, HEAD_DIM // 2, axis=1))
            store_head_pair(refs, c * HP_PER_CHUNK + p,
                            (th * cos_ref[...] + partner * sin_ref[...]) * r[:, cols])

    qk_chunks = [(OFF_Q + c * CHUNK, cq_ref, sq_ref, (q0_ref, q1_ref, q2_ref), c)
                 for c in range(W_QKV // CHUNK)]
    qk_chunks += [(OFF_K + c * CHUNK, ck_ref, sk_ref, (k0_ref, k1_ref, k2_ref), c)
                  for c in range(W_QKV // CHUNK)]
    for i, ch in enumerate(qk_chunks):
        t_ref[i % 2] = proj(ch[0])
        if i > 0:
            qk_epilogue(t_ref[(i - 1) % 2], *qk_chunks[i - 1][1:])
    qk_epilogue(t_ref[(len(qk_chunks) - 1) % 2], *qk_chunks[-1][1:])

    for c in range(W_QKV // CHUNK):
        t = proj(OFF_V + c * CHUNK)
        for p in range(HP_PER_CHUNK):
            store_head_pair((v0_ref, v1_ref, v2_ref), c * HP_PER_CHUNK + p,
                            t[:, p * HP_WIDTH:(p + 1) * HP_WIDTH])

    for j in range(0, W_ATTN_OUT, CHUNK):
        z = proj(OFF_ZA + j)
        sza_ref[:, j:j + CHUNK] = (z * _sigmoid(z)).astype(bf16)
    for j in range(0, D_MODEL, CHUNK):
        sgc_ref[:, j:j + CHUNK] = _sigmoid(proj(OFF_GC + j)).astype(bf16)
        sga_ref[:, j:j + CHUNK] = _sigmoid(proj(OFF_GA + j)).astype(bf16)


def _in_proj(x, norm_g, w_in, tables, ones_bd):
    B, S, _ = x.shape
    nt = S // TM_IN
    const = lambda b, t: (0, 0)
    tok = lambda b, t: (b, t, 0)
    sq = pl.Squeezed()
    tok_spec = lambda w: pl.BlockSpec((sq, TM_IN, w), tok)
    tok_shape = lambda w: jax.ShapeDtypeStruct((B, S, w), bf16)
    grp_shapes = tuple(jax.ShapeDtypeStruct((B, N_HP_GROUP, S // d, d * HP_WIDTH), bf16)
                       for d in DILATIONS)
    grp_specs = tuple(pl.BlockSpec((sq, N_HP_GROUP, TM_IN // d, d * HP_WIDTH),
                                   lambda b, t: (b, 0, t, 0)) for d in DILATIONS)
    return pl.pallas_call(
        _in_proj_body,
        grid=(B, nt),
        in_specs=[
            tok_spec(D_MODEL),
            pl.BlockSpec((1, D_MODEL), const),
            pl.BlockSpec((D_MODEL, IN_WIDTH), const, pipeline_mode=pl.Buffered(1)),
        ] + [pl.BlockSpec((TM_IN, HP_WIDTH), lambda b, t: (t, 0)) for _ in range(4)] + [
            pl.BlockSpec((CHUNK, CHUNK), const),
        ],
        out_specs=(tok_spec(W_CONV), tok_spec(W_CONV)) + grp_specs * 3
                  + (tok_spec(W_ATTN_OUT), tok_spec(D_MODEL), tok_spec(D_MODEL)),
        out_shape=(tok_shape(W_CONV), tok_shape(W_CONV)) + grp_shapes * 3
                  + (tok_shape(W_ATTN_OUT), tok_shape(D_MODEL), tok_shape(D_MODEL)),
        scratch_shapes=[pltpu.VMEM((TM_IN, D_MODEL), bf16),
                        pltpu.VMEM((N_PERM, PERM_ROWS, HP_WIDTH), f32),
                        pltpu.VMEM((2, TM_IN, CHUNK), f32)],
        compiler_params=pltpu.CompilerParams(
            dimension_semantics=("arbitrary", "arbitrary"),
            vmem_limit_bytes=VMEM_LIMIT_IN),
        name="in_proj",
    )(x, norm_g, w_in, *tables, ones_bd)


def _scores(q, ks):
    lane = lax.broadcasted_iota(jnp.int32, (BQ, HP_WIDTH), 1)
    head0 = lane < HEAD_DIM
    zero = jnp.zeros_like(q)
    q2 = jnp.concatenate([jnp.where(head0, q, zero), jnp.where(head0, zero, q)], axis=0)
    return lax.dot_general(q2, ks, (((1,), (1,)), ((), ())), preferred_element_type=f32)


def _softmax_pv(s, vs, bias):
    lane = lax.broadcasted_iota(jnp.int32, (BQ, HP_WIDTH), 1)
    head0 = lane < HEAD_DIM
    s = s + bias
    m = jnp.max(s, axis=1, keepdims=True)
    p = jnp.exp2(s - m).astype(bf16)
    vs1 = jnp.concatenate([vs, jnp.ones_like(vs)], axis=1)
    pv = jnp.dot(p, vs1, preferred_element_type=f32)
    acc = jnp.where(head0, pv[:BQ, :HP_WIDTH], pv[BQ:, :HP_WIDTH])
    l_sel = jnp.where(head0, pv[:BQ, HP_WIDTH:], pv[BQ:, HP_WIDTH:])
    m_sel = jnp.where(head0, m[:BQ], m[BQ:])
    return acc, m_sel, l_sel


def _band_attn_body(bias_ref, q0_ref, k0_ref, v0_ref, q1_ref, k1_ref, v1_ref,
                    q2_ref, k2_ref, v2_ref, o_ref,
                    o0_s, m0_s, l0_s, o1_s, m1_s, l1_s, o2_s, m2_s, l2_s, s_scr):
    def addr(n, L):
        if isinstance(n, int):
            qs = n * BQ
            st = min(max(qs - BAND_HALF, 0), L - SPAN)
            return qs, st, (qs - st) // BAND_HALF
        qs = pl.multiple_of(n * BQ, BQ)
        st = pl.multiple_of(jnp.clip(n * BQ - BAND_HALF, 0, L - SPAN), BAND_HALF)
        return qs, st, lax.shift_right_logical(qs - st, BAND_HALF.bit_length() - 1)

    def scores_stage(blocks, slot):
        for j, (q_ref, k_ref, _, cols, n, L, _) in enumerate(blocks):
            qs, st, _ = addr(n, L)
            s_scr[slot, j] = _scores(q_ref[pl.ds(qs, BQ), cols], k_ref[pl.ds(st, SPAN), cols])

    def softmax_stage(blocks, slot, store):
        for j, (_, _, v_ref, cols, n, L, r) in enumerate(blocks):
            qs, st, case = addr(n, L)
            acc, m, l = _softmax_pv(s_scr[slot, j], v_ref[pl.ds(st, SPAN), cols], bias_ref[case])
            store(qs, acc, m, l, r)

    def run_group(trips, blocks_of, store, next_blocks):
        assert trips % STEP_ITERS == 0 and STEP_ITERS % 2 == 0

        def iters(i, last):
            for k in range(STEP_ITERS):
                nxt = blocks_of(i + k + 1) if not (last and k == STEP_ITERS - 1) else next_blocks
                if nxt is not None:
                    scores_stage(nxt, (k + 1) % 2)
                softmax_stage(blocks_of(i + k), k % 2, store)

        def step(h, carry):
            iters(STEP_ITERS * h, False)
            return carry

        lax.fori_loop(0, trips // STEP_ITERS - 1, step, 0)
        iters(trips - STEP_ITERS, True)

    d_last = DILATIONS[-1]
    pitch = o_ref.shape[0] // d_last + RES_PAD

    def make_store(o_s, m_s, l_s, d):
        def store(qs, acc, m, l, r):
            if d == 1:
                rows = pl.ds(qs, BQ)
            elif d == d_last:
                rows = pl.ds(pl.multiple_of(r * pitch + qs, 8), BQ)
            else:
                rows = pl.ds(qs * d + r, BQ, stride=d)
            o_s[rows, :] = acc
            m_s[rows, :] = m
            l_s[rows, :] = l
        return store

    def blocks_of_group(q_ref, k_ref, v_ref, d):
        L = q_ref.shape[0]
        nb = L // BQ
        if d == 1:
            return nb // PIPE_BLOCKS, lambda i: [
                (q_ref, k_ref, v_ref, slice(None), PIPE_BLOCKS * i + j, L, 0) for j in range(PIPE_BLOCKS)]
        if d <= PIPE_BLOCKS:
            return nb, lambda i: [
                (q_ref, k_ref, v_ref, slice(r * HP_WIDTH, (r + 1) * HP_WIDTH), i, L, r) for r in range(d)]
        per_iter = PIPE_BLOCKS // nb

        def blocks(i):
            out = []
            for jr in range(per_iter):
                r = i * per_iter + jr
                cols = pl.ds(pl.multiple_of(r * HP_WIDTH, HP_WIDTH), HP_WIDTH)
                out += [(q_ref, k_ref, v_ref, cols, n, L, r) for n in range(nb)]
            return out
        return d // per_iter, blocks

    groups = []
    for (q_ref, k_ref, v_ref), o_s, m_s, l_s, d in (
            ((q0_ref, k0_ref, v0_ref), o0_s, m0_s, l0_s, DILATIONS[0]),
            ((q1_ref, k1_ref, v1_ref), o1_s, m1_s, l1_s, DILATIONS[1]),
            ((q2_ref, k2_ref, v2_ref), o2_s, m2_s, l2_s, DILATIONS[2])):
        trips, blocks = blocks_of_group(q_ref, k_ref, v_ref, d)
        groups.append((trips, blocks, make_store(o_s, m_s, l_s, d)))
    scores_stage(groups[0][1](0), 0)
    for g, (trips, blocks, store) in enumerate(groups):
        run_group(trips, blocks, store, groups[g + 1][1](0) if g + 1 < len(groups) else None)

    def merge(c, carry):
        rows = pl.ds(pl.multiple_of(c * BQ, BQ), BQ)

        def last_group(ref):
            base = c * (BQ // d_last)
            return jnp.concatenate([ref[pl.ds(base + a, d_last, stride=pitch), :]
                                    for a in range(BQ // d_last)], axis=0)

        m0, m1, m2 = m0_s[rows, :], m1_s[rows, :], last_group(m2_s)
        mx = jnp.maximum(jnp.maximum(m0, m1), m2)
        w0, w1, w2 = jnp.exp2(m0 - mx), jnp.exp2(m1 - mx), jnp.exp2(m2 - mx)
        num = w0 * o0_s[rows, :] + w1 * o1_s[rows, :] + w2 * last_group(o2_s)
        den = w0 * l0_s[rows, :] + w1 * l1_s[rows, :] + w2 * last_group(l2_s)
        o_ref[rows, :] = (num * (1.0 / den)).astype(bf16)
        return carry

    lax.fori_loop(0, o_ref.shape[0] // BQ, merge, 0, unroll=4)


def _band_bias():
    qi = np.arange(2 * BQ)[:, None] % BQ
    kj = np.arange(SPAN)[None, :]
    rel = np.stack([kj - qi - c * BAND_HALF for c in range(N_BIAS)])
    return jnp.asarray(np.where(np.abs(rel) <= BAND_HALF, 0.0, NEG_BIG), dtype=f32)


def _band_attn(qkv):
    B = qkv[0].shape[0]
    S = qkv[0].shape[2]
    sq = pl.Squeezed()
    hp_block = lambda b, h: (b, h, 0, 0)
    operands, specs = [], []
    for g, d in enumerate(DILATIONS):
        for a in range(3):
            operands.append(qkv[a * N_GROUPS + g])
            specs.append(pl.BlockSpec((sq, sq, S // d, d * HP_WIDTH), hp_block))
    return pl.pallas_call(
        _band_attn_body,
        grid=(B, N_HP_GROUP),
        in_specs=[pl.BlockSpec((N_BIAS, 2 * BQ, SPAN), lambda b, h: (0, 0, 0))] + specs,
        out_specs=pl.BlockSpec((sq, sq, S, HP_WIDTH), hp_block),
        out_shape=jax.ShapeDtypeStruct((B, N_HP_GROUP, S, HP_WIDTH), bf16),
        scratch_shapes=[pltpu.VMEM((S, HP_WIDTH), f32) for _ in range(3 * (N_GROUPS - 1))]
                       + [pltpu.VMEM((S + DILATIONS[-1] * RES_PAD, HP_WIDTH), f32) for _ in range(3)]
                       + [pltpu.VMEM((2, PIPE_BLOCKS, 2 * BQ, SPAN), f32)],
        compiler_params=pltpu.CompilerParams(
            dimension_semantics=("arbitrary", "arbitrary"),
            vmem_limit_bytes=VMEM_LIMIT_ATTN),
        name="band_attn",
    )(_band_bias(), *operands)


def _out_proj_body(x_ref, u_ref, up_ref, un_ref, a_ref, o_ref, sza_ref, sgc_ref, sga_ref,
                   cw_ref, cb_ref, wbc_ref, wba_ref, wo_ref, out_ref):
    t = pl.program_id(1)
    nt = pl.num_programs(1)
    u = u_ref[...].astype(f32)
    prev_row = up_ref[...].astype(f32)[BF16_ROWS - 1:BF16_ROWS] * jnp.where(t > 0, 1.0, 0.0)
    next_row = un_ref[...].astype(f32)[0:1] * jnp.where(t < nt - 1, 1.0, 0.0)
    row = lax.broadcasted_iota(jnp.int32, (F32_ROWS, 1), 0)
    rolled = pltpu.roll(u, 1, axis=0)
    u_prev = jnp.concatenate([jnp.where(row == 0, prev_row, rolled[:F32_ROWS]), rolled[F32_ROWS:]], axis=0)
    rolled = pltpu.roll(u, TM_OUT - 1, axis=0)
    u_next = jnp.concatenate([rolled[:-F32_ROWS],
                              jnp.where(row == F32_ROWS - 1, next_row, rolled[-F32_ROWS:])], axis=0)
    o_cat = jnp.concatenate([o_ref[h] for h in range(N_HP_GROUP)], axis=1)
    p_a = jnp.dot(o_cat * sza_ref[...], wba_ref[...], preferred_element_type=f32)
    p_c = None
    for j in range(0, W_CONV, CHUNK):
        cols = slice(j, j + CHUNK)
        conv = (u_prev[:, cols] * cw_ref[0:1, cols] + u[:, cols] * cw_ref[1:2, cols]
                + u_next[:, cols] * cw_ref[2:3, cols] + cb_ref[:, cols])
        y_c = (a_ref[:, cols].astype(f32) * conv).astype(bf16)
        part = jnp.dot(y_c, wbc_ref[cols, :], preferred_element_type=f32)
        p_c = part if p_c is None else p_c + part
    m = (sgc_ref[...].astype(f32) * p_c + sga_ref[...].astype(f32) * p_a).astype(bf16)
    out_ref[...] = x_ref[...] + jnp.dot(m, wo_ref[...], preferred_element_type=f32)


def _out_proj(x, u, a, o, sza, sgc, sga, conv_w, conv_b, wbc, wba, wo):
    B, S, _ = x.shape
    nt = S // TM_OUT
    rb = TM_OUT // BF16_ROWS
    sq = pl.Squeezed()
    const = lambda b, t: (0, 0)
    tok = lambda b, t: (b, t, 0)
    tok_spec = lambda w: pl.BlockSpec((sq, TM_OUT, w), tok)
    halo_prev = pl.BlockSpec((sq, BF16_ROWS, W_CONV), lambda b, t: (b, jnp.maximum(t * rb - 1, 0), 0))
    halo_next = pl.BlockSpec((sq, BF16_ROWS, W_CONV),
                             lambda b, t: (b, jnp.minimum((t + 1) * rb, S // BF16_ROWS - 1), 0))
    return pl.pallas_call(
        _out_proj_body,
        grid=(B, nt),
        in_specs=[
            tok_spec(D_MODEL), tok_spec(W_CONV), halo_prev, halo_next, tok_spec(W_CONV),
            pl.BlockSpec((sq, N_HP_GROUP, TM_OUT, HP_WIDTH), lambda b, t: (b, 0, t, 0)),
            tok_spec(W_ATTN_OUT), tok_spec(D_MODEL), tok_spec(D_MODEL),
            pl.BlockSpec((3, W_CONV), const), pl.BlockSpec((1, W_CONV), const),
            pl.BlockSpec((W_CONV, D_MODEL), const), pl.BlockSpec((W_ATTN_OUT, D_MODEL), const),
            pl.BlockSpec((D_MODEL, D_MODEL), const),
        ],
        out_specs=tok_spec(D_MODEL),
        out_shape=jax.ShapeDtypeStruct((B, S, D_MODEL), f32),
        compiler_params=pltpu.CompilerParams(
            dimension_semantics=("arbitrary", "arbitrary"),
            vmem_limit_bytes=VMEM_LIMIT_OUT),
        name="out_proj",
    )(x, u, u, u, a, o, sza, sgc, sga, conv_w, conv_b, wbc, wba, wo)


def _rotary_tables(S, q_gain, k_gain):
    half = HEAD_DIM // 2
    inv_freq = ROPE_THETA ** (-jnp.arange(0, half, dtype=f32) / half)
    ang = jnp.arange(S, dtype=f32)[:, None] * inv_freq[None, :]
    cos, sin = jnp.cos(ang), jnp.sin(ang)
    reps = HP_WIDTH // HEAD_DIM
    cos_t = jnp.tile(jnp.concatenate([cos, cos], axis=1), (1, reps))
    sin_t = jnp.tile(jnp.concatenate([-sin, sin], axis=1), (1, reps))
    tables = []
    for g, scale in ((q_gain, LOG2E / math.sqrt(HEAD_DIM)), (k_gain, 1.0)):
        g = g.astype(f32) * scale
        g_swap = jnp.concatenate([g[half:], g[:half]])
        tables += [cos_t * jnp.tile(g, reps)[None, :], sin_t * jnp.tile(g_swap, reps)[None, :]]
    return tables


def kernel(x, norm_g, w_in, conv_w, conv_b, q_norm_g, k_norm_g, w_branch_conv, w_branch_attn, w_out):
    B, S, D = x.shape
    assert D == D_MODEL and w_in.shape == (D_MODEL, IN_WIDTH)
    assert S % TM_IN == 0 and S % TM_OUT == 0 and (S // DILATIONS[-1]) % BQ == 0
    assert (TM_IN // DILATIONS[-1]) % BF16_ROWS == 0 and (S // BQ) % PIPE_BLOCKS == 0
    head_id = np.arange(CHUNK) // HEAD_DIM
    ones_bd = jnp.asarray(head_id[:, None] == head_id[None, :], dtype=bf16)

    outs = _in_proj(x, norm_g.astype(f32)[None, :], w_in.astype(bf16),
                    _rotary_tables(S, q_norm_g, k_norm_g), ones_bd)
    u, a = outs[0], outs[1]
    sza, sgc, sga = outs[11], outs[12], outs[13]
    o = _band_attn(outs[2:11])
    return _out_proj(x, u, a, o, sza, sgc, sga, conv_w.astype(f32), conv_b.astype(f32)[None, :],
                     w_branch_conv.astype(bf16), w_branch_attn.astype(bf16), w_out.astype(bf16))
```

```python
import math

import jax
import jax.numpy as jnp
import numpy as np
from jax import lax
from jax.experimental import pallas as pl
from jax.experimental.pallas import tpu as pltpu

D_MODEL = 1024
W_CONV = 1024
HEAD_DIM = 64
HEADS_PER_GROUP = 8
DILATIONS = (1, 4, 16)
BAND_HALF = 64
N_GROUPS = 3
W_QKV = N_GROUPS * HEADS_PER_GROUP * HEAD_DIM
W_ATTN_OUT = HEADS_PER_GROUP * HEAD_DIM
ROPE_THETA = 10000.0
NORM_EPS = 1e-6

OFF_B, OFF_C, OFF_H, OFF_ZC = 0, 1024, 2048, 3072
OFF_Q = 4096
OFF_K = OFF_Q + W_QKV
OFF_V = OFF_K + W_QKV
OFF_ZA = OFF_V + W_QKV
OFF_GC = OFF_ZA + W_ATTN_OUT
OFF_GA = OFF_GC + D_MODEL
IN_WIDTH = OFF_GA + D_MODEL

MXU_WIDTH = 256
HP_WIDTH = 2 * HEAD_DIM
N_HP_GROUP = HEADS_PER_GROUP // 2
F32_ROWS = 8
BF16_ROWS = 16

TM_IN = 512
TM_OUT = 512
CHUNK = MXU_WIDTH
HP_PER_CHUNK = CHUNK // HP_WIDTH
W_STAGE = 512
PERM_FREE_STRIDE = 4
N_PERM = 4
BQ = 128
SPAN = BQ + 2 * BAND_HALF
RES_PAD = 8
STEP_ITERS = 4
PIPE_BLOCKS = 4
N_BIAS = SPAN // BAND_HALF - 1
NEG_BIG = -1e30
LOG2E = math.log2(math.e)

VMEM_LIMIT_IN = 58 * 1024 * 1024
VMEM_LIMIT_ATTN = 48 * 1024 * 1024
VMEM_LIMIT_OUT = 48 * 1024 * 1024

f32 = jnp.float32
bf16 = jnp.bfloat16


def _perm_pitch(d):
    return d if d <= PERM_FREE_STRIDE else d + F32_ROWS


PERM_ROWS = max(TM_IN // d * _perm_pitch(d) for d in DILATIONS)


def _sigmoid(z):
    return 1.0 / (1.0 + jnp.exp(-z))


def _w_stage_copy(w_hbm, stage_ref, sem_ref, c):
    slot = c % 2
    return pltpu.make_async_copy(w_hbm.at[:, pl.ds(c * W_STAGE, W_STAGE)], stage_ref.at[slot],
                                 sem_ref.at[slot])


def _in_proj_body(x_ref, ng_ref, w_hbm, cq_ref, sq_ref, ck_ref, sk_ref, ones_ref,
                  u_ref, a_ref, q0_ref, q1_ref, q2_ref, k0_ref, k1_ref, k2_ref,
                  v0_ref, v1_ref, v2_ref, sza_ref, sgc_ref, sga_ref,
                  xn_ref, perm_ref, t_ref, w_ref, stage_ref, sem_ref):
    @pl.when((pl.program_id(0) == 0) & (pl.program_id(1) == 0))
    def _():
        n_stage = IN_WIDTH // W_STAGE
        _w_stage_copy(w_hbm, stage_ref, sem_ref, 0).start()
        for c in range(n_stage):
            if c + 1 < n_stage:
                _w_stage_copy(w_hbm, stage_ref, sem_ref, c + 1).start()
            _w_stage_copy(w_hbm, stage_ref, sem_ref, c).wait()
            w_ref[:, c * W_STAGE:(c + 1) * W_STAGE] = stage_ref[c % 2].astype(bf16)

    x = x_ref[...]
    ms = jnp.mean(x * x, axis=-1, keepdims=True)
    xn_ref[...] = (x * lax.rsqrt(ms + NORM_EPS) * ng_ref[...]).astype(bf16)

    def proj(c0):
        return jnp.dot(xn_ref[...], w_ref[:, c0:c0 + CHUNK], preferred_element_type=f32)

    n_perm_used = [0]

    def store_head_pair(group_refs, hp, val):
        g, h = divmod(hp, N_HP_GROUP)
        d = DILATIONS[g]
        if d == 1:
            group_refs[g][h] = val.astype(bf16)
            return
        buf = perm_ref.at[n_perm_used[0] % N_PERM]
        n_perm_used[0] += 1
        pitch = _perm_pitch(d)
        if pitch == d:
            buf[0:TM_IN, :] = val
        else:
            for j in range(TM_IN // d):
                buf[j * pitch:j * pitch + d, :] = val[j * d:(j + 1) * d, :]
        for r in range(d):
            group_refs[g][h, :, r * HP_WIDTH:(r + 1) * HP_WIDTH] = (
                buf[pl.ds(r, TM_IN // d, stride=pitch), :].astype(bf16))

    for j in range(0, W_CONV, CHUNK):
        u_ref[:, j:j + CHUNK] = (proj(OFF_C + j) * proj(OFF_H + j)).astype(bf16)
        z = proj(OFF_ZC + j)
        a_ref[:, j:j + CHUNK] = (proj(OFF_B + j) * z * _sigmoid(z)).astype(bf16)

    lane = lax.broadcasted_iota(jnp.int32, (1, HP_WIDTH), 1)
    first_half = (lane % HEAD_DIM) < (HEAD_DIM // 2)

    def qk_epilogue(t, cos_ref, sin_ref, refs, c):
        ss = jnp.dot((t * t).astype(bf16), ones_ref[...], preferred_element_type=f32)
        r = lax.rsqrt(ss * (1.0 / HEAD_DIM) + NORM_EPS)
        for p in range(HP_PER_CHUNK):
            cols = slice(p * HP_WIDTH, (p + 1) * HP_WIDTH)
            th = t[:, cols]
            partner = jnp.where(first_half,
                                pltpu.roll(th, HP_WIDTH - HEAD_DIM // 2, axis=1),
                                pltpu.roll(th, HEAD_DIM // 2, axis=1))
            store_head_pair(refs, c * HP_PER_CHUNK + p,
                            (th * cos_ref[...] + partner * sin_ref[...]) * r[:, cols])

    qk_chunks = [(OFF_Q + c * CHUNK, cq_ref, sq_ref, (q0_ref, q1_ref, q2_ref), c)
                 for c in range(W_QKV // CHUNK)]
    qk_chunks += [(OFF_K + c * CHUNK, ck_ref, sk_ref, (k0_ref, k1_ref, k2_ref), c)
                  for c in range(W_QKV // CHUNK)]
    for i, ch in enumerate(qk_chunks):
        t_ref[i % 2] = proj(ch[0])
        if i > 0:
            qk_epilogue(t_ref[(i - 1) % 2], *qk_chunks[i - 1][1:])
    qk_epilogue(t_ref[(len(qk_chunks) - 1) % 2], *qk_chunks[-1][1:])

    for c in range(W_QKV // CHUNK):
        t = proj(OFF_V + c * CHUNK)
        for p in range(HP_PER_CHUNK):
            store_head_pair((v0_ref, v1_ref, v2_ref), c * HP_PER_CHUNK + p,
                            t[:, p * HP_WIDTH:(p + 1) * HP_WIDTH])

    for j in range(0, W_ATTN_OUT, CHUNK):
        z = proj(OFF_ZA + j)
        sza_ref[:, j:j + CHUNK] = (z * _sigmoid(z)).astype(bf16)
    for j in range(0, D_MODEL, CHUNK):
        sgc_ref[:, j:j + CHUNK] = _sigmoid(proj(OFF_GC + j)).astype(bf16)
        sga_ref[:, j:j + CHUNK] = _sigmoid(proj(OFF_GA + j)).astype(bf16)


def _in_proj(x, norm_g, w_in, tables, ones_bd):
    B, S, _ = x.shape
    nt = S // TM_IN
    const = lambda b, t: (0, 0)
    tok = lambda b, t: (b, t, 0)
    sq = pl.Squeezed()
    tok_spec = lambda w: pl.BlockSpec((sq, TM_IN, w), tok)
    tok_shape = lambda w: jax.ShapeDtypeStruct((B, S, w), bf16)
    grp_shapes = tuple(jax.ShapeDtypeStruct((B, N_HP_GROUP, S // d, d * HP_WIDTH), bf16)
                       for d in DILATIONS)
    grp_specs = tuple(pl.BlockSpec((sq, N_HP_GROUP, TM_IN // d, d * HP_WIDTH),
                                   lambda b, t: (b, 0, t, 0)) for d in DILATIONS)
    return pl.pallas_call(
        _in_proj_body,
        grid=(B, nt),
        in_specs=[
            tok_spec(D_MODEL),
            pl.BlockSpec((1, D_MODEL), const),
            pl.BlockSpec(memory_space=pl.ANY),
        ] + [pl.BlockSpec((TM_IN, HP_WIDTH), lambda b, t: (t, 0)) for _ in range(4)] + [
            pl.BlockSpec((CHUNK, CHUNK), const),
        ],
        out_specs=(tok_spec(W_CONV), tok_spec(W_CONV)) + grp_specs * 3
                  + (tok_spec(W_ATTN_OUT), tok_spec(D_MODEL), tok_spec(D_MODEL)),
        out_shape=(tok_shape(W_CONV), tok_shape(W_CONV)) + grp_shapes * 3
                  + (tok_shape(W_ATTN_OUT), tok_shape(D_MODEL), tok_shape(D_MODEL)),
        scratch_shapes=[pltpu.VMEM((TM_IN, D_MODEL), bf16),
                        pltpu.VMEM((N_PERM, PERM_ROWS, HP_WIDTH), f32),
                        pltpu.VMEM((2, TM_IN, CHUNK), f32),
                        pltpu.VMEM((D_MODEL, IN_WIDTH), bf16),
                        pltpu.VMEM((2, D_MODEL, W_STAGE), f32),
                        pltpu.SemaphoreType.DMA((2,))],
        compiler_params=pltpu.CompilerParams(
            dimension_semantics=("arbitrary", "arbitrary"),
            vmem_limit_bytes=VMEM_LIMIT_IN),
        name="in_proj",
    )(x, norm_g, w_in, *tables, ones_bd)


def _scores(q, ks):
    lane = lax.broadcasted_iota(jnp.int32, (BQ, HP_WIDTH), 1)
    head0 = lane < HEAD_DIM
    zero = jnp.zeros_like(q)
    q2 = jnp.concatenate([jnp.where(head0, q, zero), jnp.where(head0, zero, q)], axis=0)
    return lax.dot_general(q2, ks, (((1,), (1,)), ((), ())), preferred_element_type=f32)


def _softmax_pv(s, vs, bias):
    lane = lax.broadcasted_iota(jnp.int32, (BQ, HP_WIDTH), 1)
    head0 = lane < HEAD_DIM
    s = s + bias
    m = jnp.max(s, axis=1, keepdims=True)
    p = jnp.exp2(s - m).astype(bf16)
    vs1 = jnp.concatenate([vs, jnp.ones_like(vs)], axis=1)
    pv = jnp.dot(p, vs1, preferred_element_type=f32)
    acc = jnp.where(head0, pv[:BQ, :HP_WIDTH], pv[BQ:, :HP_WIDTH])
    l_sel = jnp.where(head0, pv[:BQ, HP_WIDTH:], pv[BQ:, HP_WIDTH:])
    m_sel = jnp.where(head0, m[:BQ], m[BQ:])
    return acc, m_sel, l_sel


def _band_attn_body(bias_ref, q0_ref, k0_ref, v0_ref, q1_ref, k1_ref, v1_ref,
                    q2_ref, k2_ref, v2_ref, o_ref,
                    o0_s, m0_s, l0_s, o1_s, m1_s, l1_s, o2_s, m2_s, l2_s, s_scr):
    def addr(n, L):
        if isinstance(n, int):
            qs = n * BQ
            st = min(max(qs - BAND_HALF, 0), L - SPAN)
            return qs, st, (qs - st) // BAND_HALF
        qs = pl.multiple_of(n * BQ, BQ)
        st = pl.multiple_of(jnp.clip(n * BQ - BAND_HALF, 0, L - SPAN), BAND_HALF)
        return qs, st, lax.shift_right_logical(qs - st, BAND_HALF.bit_length() - 1)

    def scores_stage(blocks, slot):
        for j, (q_ref, k_ref, _, cols, n, L, _) in enumerate(blocks):
            qs, st, _ = addr(n, L)
            s_scr[slot, j] = _scores(q_ref[pl.ds(qs, BQ), cols], k_ref[pl.ds(st, SPAN), cols])

    def softmax_stage(blocks, slot, store):
        for j, (_, _, v_ref, cols, n, L, r) in enumerate(blocks):
            qs, st, case = addr(n, L)
            acc, m, l = _softmax_pv(s_scr[slot, j], v_ref[pl.ds(st, SPAN), cols], bias_ref[case])
            store(qs, acc, m, l, r)

    def run_group(trips, blocks_of, store, next_blocks):
        assert trips % STEP_ITERS == 0 and STEP_ITERS % 2 == 0

        def iters(i, last):
            for k in range(STEP_ITERS):
                nxt = blocks_of(i + k + 1) if not (last and k == STEP_ITERS - 1) else next_blocks
                if nxt is not None:
                    scores_stage(nxt, (k + 1) % 2)
                softmax_stage(blocks_of(i + k), k % 2, store)

        def step(h, carry):
            iters(STEP_ITERS * h, False)
            return carry

        lax.fori_loop(0, trips // STEP_ITERS - 1, step, 0)
        iters(trips - STEP_ITERS, True)

    d_last = DILATIONS[-1]
    pitch = o_ref.shape[0] // d_last + RES_PAD

    def make_store(o_s, m_s, l_s, d):
        def store(qs, acc, m, l, r):
            if d == 1:
                rows = pl.ds(qs, BQ)
            elif d == d_last:
                rows = pl.ds(pl.multiple_of(r * pitch + qs, 8), BQ)
            else:
                rows = pl.ds(qs * d + r, BQ, stride=d)
            o_s[rows, :] = acc
            m_s[rows, :] = m
            l_s[rows, :] = l
        return store

    def blocks_of_group(q_ref, k_ref, v_ref, d):
        L = q_ref.shape[0]
        nb = L // BQ
        if d == 1:
            return nb // PIPE_BLOCKS, lambda i: [
                (q_ref, k_ref, v_ref, slice(None), PIPE_BLOCKS * i + j, L, 0) for j in range(PIPE_BLOCKS)]
        if d <= PIPE_BLOCKS:
            return nb, lambda i: [
                (q_ref, k_ref, v_ref, slice(r * HP_WIDTH, (r + 1) * HP_WIDTH), i, L, r) for r in range(d)]
        per_iter = PIPE_BLOCKS // nb

        def blocks(i):
            out = []
            for jr in range(per_iter):
                r = i * per_iter + jr
                cols = pl.ds(pl.multiple_of(r * HP_WIDTH, HP_WIDTH), HP_WIDTH)
                out += [(q_ref, k_ref, v_ref, cols, n, L, r) for n in range(nb)]
            return out
        return d // per_iter, blocks

    groups = []
    for (q_ref, k_ref, v_ref), o_s, m_s, l_s, d in (
            ((q0_ref, k0_ref, v0_ref), o0_s, m0_s, l0_s, DILATIONS[0]),
            ((q1_ref, k1_ref, v1_ref), o1_s, m1_s, l1_s, DILATIONS[1]),
            ((q2_ref, k2_ref, v2_ref), o2_s, m2_s, l2_s, DILATIONS[2])):
        trips, blocks = blocks_of_group(q_ref, k_ref, v_ref, d)
        groups.append((trips, blocks, make_store(o_s, m_s, l_s, d)))
    scores_stage(groups[0][1](0), 0)
    for g, (trips, blocks, store) in enumerate(groups):
        run_group(trips, blocks, store, groups[g + 1][1](0) if g + 1 < len(groups) else None)

    def merge(c, carry):
        rows = pl.ds(pl.multiple_of(c * BQ, BQ), BQ)

        def last_group(ref):
            base = c * (BQ // d_last)
            return jnp.concatenate([ref[pl.ds(base + a, d_last, stride=pitch), :]
                                    for a in range(BQ // d_last)], axis=0)

        m0, m1, m2 = m0_s[rows, :], m1_s[rows, :], last_group(m2_s)
        mx = jnp.maximum(jnp.maximum(m0, m1), m2)
        w0, w1, w2 = jnp.exp2(m0 - mx), jnp.exp2(m1 - mx), jnp.exp2(m2 - mx)
        num = w0 * o0_s[rows, :] + w1 * o1_s[rows, :] + w2 * last_group(o2_s)
        den = w0 * l0_s[rows, :] + w1 * l1_s[rows, :] + w2 * last_group(l2_s)
        o_ref[rows, :] = (num * (1.0 / den)).astype(bf16)
        return carry

    lax.fori_loop(0, o_ref.shape[0] // BQ, merge, 0, unroll=4)


def _band_bias():
    qi = np.arange(2 * BQ)[:, None] % BQ
    kj = np.arange(SPAN)[None, :]
    rel = np.stack([kj - qi - c * BAND_HALF for c in range(N_BIAS)])
    return jnp.asarray(np.where(np.abs(rel) <= BAND_HALF, 0.0, NEG_BIG), dtype=f32)


def _band_attn(qkv):
    B = qkv[0].shape[0]
    S = qkv[0].shape[2]
    sq = pl.Squeezed()
    hp_block = lambda b, h: (b, h, 0, 0)
    operands, specs = [], []
    for g, d in enumerate(DILATIONS):
        for a in range(3):
            operands.append(qkv[a * N_GROUPS + g])
            specs.append(pl.BlockSpec((sq, sq, S // d, d * HP_WIDTH), hp_block))
    return pl.pallas_call(
        _band_attn_body,
        grid=(B, N_HP_GROUP),
        in_specs=[pl.BlockSpec((N_BIAS, 2 * BQ, SPAN), lambda b, h: (0, 0, 0))] + specs,
        out_specs=pl.BlockSpec((sq, sq, S, HP_WIDTH), hp_block),
        out_shape=jax.ShapeDtypeStruct((B, N_HP_GROUP, S, HP_WIDTH), bf16),
        scratch_shapes=[pltpu.VMEM((S, HP_WIDTH), f32) for _ in range(3 * (N_GROUPS - 1))]
                       + [pltpu.VMEM((S + DILATIONS[-1] * RES_PAD, HP_WIDTH), f32) for _ in range(3)]
                       + [pltpu.VMEM((2, PIPE_BLOCKS, 2 * BQ, SPAN), f32)],
        compiler_params=pltpu.CompilerParams(
            dimension_semantics=("arbitrary", "arbitrary"),
            vmem_limit_bytes=VMEM_LIMIT_ATTN),
        name="band_attn",
    )(_band_bias(), *operands)


def _out_proj_body(x_ref, u_ref, up_ref, un_ref, a_ref, o_ref, sza_ref, sgc_ref, sga_ref,
                   cw_ref, cb_ref, wbc_ref, wba_ref, wo_ref, out_ref):
    t = pl.program_id(1)
    nt = pl.num_programs(1)
    u = u_ref[...].astype(f32)
    prev_row = up_ref[...].astype(f32)[BF16_ROWS - 1:BF16_ROWS] * jnp.where(t > 0, 1.0, 0.0)
    next_row = un_ref[...].astype(f32)[0:1] * jnp.where(t < nt - 1, 1.0, 0.0)
    row = lax.broadcasted_iota(jnp.int32, (F32_ROWS, 1), 0)
    rolled = pltpu.roll(u, 1, axis=0)
    u_prev = jnp.concatenate([jnp.where(row == 0, prev_row, rolled[:F32_ROWS]), rolled[F32_ROWS:]], axis=0)
    rolled = pltpu.roll(u, TM_OUT - 1, axis=0)
    u_next = jnp.concatenate([rolled[:-F32_ROWS],
                              jnp.where(row == F32_ROWS - 1, next_row, rolled[-F32_ROWS:])], axis=0)
    o_cat = jnp.concatenate([o_ref[h] for h in range(N_HP_GROUP)], axis=1)
    p_a = jnp.dot(o_cat * sza_ref[...], wba_ref[...], preferred_element_type=f32)
    p_c = None
    for j in range(0, W_CONV, CHUNK):
        cols = slice(j, j + CHUNK)
        conv = (u_prev[:, cols] * cw_ref[0:1, cols] + u[:, cols] * cw_ref[1:2, cols]
                + u_next[:, cols] * cw_ref[2:3, cols] + cb_ref[:, cols])
        y_c = (a_ref[:, cols].astype(f32) * conv).astype(bf16)
        part = jnp.dot(y_c, wbc_ref[cols, :], preferred_element_type=f32)
        p_c = part if p_c is None else p_c + part
    m = (sgc_ref[...].astype(f32) * p_c + sga_ref[...].astype(f32) * p_a).astype(bf16)
    out_ref[...] = x_ref[...] + jnp.dot(m, wo_ref[...], preferred_element_type=f32)


def _out_proj(x, u, a, o, sza, sgc, sga, conv_w, conv_b, wbc, wba, wo):
    B, S, _ = x.shape
    nt = S // TM_OUT
    rb = TM_OUT // BF16_ROWS
    sq = pl.Squeezed()
    const = lambda b, t: (0, 0)
    tok = lambda b, t: (b, t, 0)
    tok_spec = lambda w: pl.BlockSpec((sq, TM_OUT, w), tok)
    halo_prev = pl.BlockSpec((sq, BF16_ROWS, W_CONV), lambda b, t: (b, jnp.maximum(t * rb - 1, 0), 0))
    halo_next = pl.BlockSpec((sq, BF16_ROWS, W_CONV),
                             lambda b, t: (b, jnp.minimum((t + 1) * rb, S // BF16_ROWS - 1), 0))
    return pl.pallas_call(
        _out_proj_body,
        grid=(B, nt),
        in_specs=[
            tok_spec(D_MODEL), tok_spec(W_CONV), halo_prev, halo_next, tok_spec(W_CONV),
            pl.BlockSpec((sq, N_HP_GROUP, TM_OUT, HP_WIDTH), lambda b, t: (b, 0, t, 0)),
            tok_spec(W_ATTN_OUT), tok_spec(D_MODEL), tok_spec(D_MODEL),
            pl.BlockSpec((3, W_CONV), const), pl.BlockSpec((1, W_CONV), const),
            pl.BlockSpec((W_CONV, D_MODEL), const), pl.BlockSpec((W_ATTN_OUT, D_MODEL), const),
            pl.BlockSpec((D_MODEL, D_MODEL), const),
        ],
        out_specs=tok_spec(D_MODEL),
        out_shape=jax.ShapeDtypeStruct((B, S, D_MODEL), f32),
        compiler_params=pltpu.CompilerParams(
            dimension_semantics=("arbitrary", "arbitrary"),
            vmem_limit_bytes=VMEM_LIMIT_OUT),
        name="out_proj",
    )(x, u, u, u, a, o, sza, sgc, sga, conv_w, conv_b, wbc, wba, wo)


def _rotary_tables(S, q_gain, k_gain):
    half = HEAD_DIM // 2
    inv_freq = ROPE_THETA ** (-jnp.arange(0, half, dtype=f32) / half)
    ang = jnp.arange(S, dtype=f32)[:, None] * inv_freq[None, :]
    cos, sin = jnp.cos(ang), jnp.sin(ang)
    reps = HP_WIDTH // HEAD_DIM
    cos_t = jnp.tile(jnp.concatenate([cos, cos], axis=1), (1, reps))
    sin_t = jnp.tile(jnp.concatenate([-sin, sin], axis=1), (1, reps))
    tables = []
    for g, scale in ((q_gain, LOG2E / math.sqrt(HEAD_DIM)), (k_gain, 1.0)):
        g = g.astype(f32) * scale
        g_swap = jnp.concatenate([g[half:], g[:half]])
        tables += [cos_t * jnp.tile(g, reps)[None, :], sin_t * jnp.tile(g_swap, reps)[None, :]]
    return tables


def kernel(x, norm_g, w_in, conv_w, conv_b, q_norm_g, k_norm_g, w_branch_conv, w_branch_attn, w_out):
    B, S, D = x.shape
    assert D == D_MODEL and w_in.shape == (D_MODEL, IN_WIDTH) and IN_WIDTH % W_STAGE == 0
    assert S % TM_IN == 0 and S % TM_OUT == 0 and (S // DILATIONS[-1]) % BQ == 0
    assert (TM_IN // DILATIONS[-1]) % BF16_ROWS == 0 and (S // BQ) % PIPE_BLOCKS == 0
    head_id = np.arange(CHUNK) // HEAD_DIM
    ones_bd = jnp.asarray(head_id[:, None] == head_id[None, :], dtype=bf16)

    outs = _in_proj(x, norm_g.astype(f32)[None, :], w_in.astype(f32),
                    _rotary_tables(S, q_norm_g, k_norm_g), ones_bd)
    u, a = outs[0], outs[1]
    sza, sgc, sga = outs[11], outs[12], outs[13]
    o = _band_attn(outs[2:11])
    return _out_proj(x, u, a, o, sza, sgc, sga, conv_w.astype(f32), conv_b.astype(f32)[None, :],
                     w_branch_conv.astype(bf16), w_branch_attn.astype(bf16), w_out.astype(bf16))
```

```python
import math

import jax
import jax.numpy as jnp
import numpy as np
from jax import lax
from jax.experimental import pallas as pl
from jax.experimental.pallas import tpu as pltpu

D_MODEL = 1024
W_CONV = 1024
HEAD_DIM = 64
HEADS_PER_GROUP = 8
DILATIONS = (1, 4, 16)
BAND_HALF = 64
N_GROUPS = 3
W_QKV = N_GROUPS * HEADS_PER_GROUP * HEAD_DIM
W_ATTN_OUT = HEADS_PER_GROUP * HEAD_DIM
ROPE_THETA = 10000.0
NORM_EPS = 1e-6

OFF_B, OFF_C, OFF_H, OFF_ZC = 0, 1024, 2048, 3072
OFF_Q = 4096
OFF_K = OFF_Q + W_QKV
OFF_V = OFF_K + W_QKV
OFF_ZA = OFF_V + W_QKV
OFF_GC = OFF_ZA + W_ATTN_OUT
OFF_GA = OFF_GC + D_MODEL
IN_WIDTH = OFF_GA + D_MODEL

LANES = 128
MXU_WIDTH = 256
HP_WIDTH = 2 * HEAD_DIM
N_HP_GROUP = HEADS_PER_GROUP // 2
CONV_TAPS = 3
EDGE_ROWS = 4
F32_ROWS = 8
BF16_ROWS = 16

TM_IN = 512
TM_OUT = 512
CHUNK = MXU_WIDTH
HP_PER_CHUNK = CHUNK // HP_WIDTH
PERM_FREE_STRIDE = 4
N_PERM = 4
BQ = 128
SPAN = BQ + 2 * BAND_HALF
RES_PAD = 8
STEP_ITERS = 4
PIPE_BLOCKS = 4
N_BIAS = SPAN // BAND_HALF - 1
NEG_BIG = -1e30
LOG2E = math.log2(math.e)

VMEM_LIMIT_IN = 58 * 1024 * 1024
VMEM_LIMIT_ATTN = 48 * 1024 * 1024
VMEM_LIMIT_OUT = 48 * 1024 * 1024

f32 = jnp.float32
bf16 = jnp.bfloat16


def _perm_pitch(d):
    return d if d <= PERM_FREE_STRIDE else d + F32_ROWS


PERM_ROWS = max(TM_IN // d * _perm_pitch(d) for d in DILATIONS)


def _sigmoid(z):
    return 1.0 / (1.0 + jnp.exp(-z))


def _in_proj_body(x_ref, ng_ref, w_ref, cq_ref, sq_ref, ck_ref, sk_ref, ones_ref, cw_ref, cb_ref,
                  yc_ref, edge_ref, q0_ref, q1_ref, q2_ref, k0_ref, k1_ref, k2_ref,
                  v0_ref, v1_ref, v2_ref, sza_ref, sgc_ref, sga_ref, xn_ref, perm_ref, t_ref):
    x = x_ref[...]
    ms = jnp.mean(x * x, axis=-1, keepdims=True)
    xn_ref[...] = (x * lax.rsqrt(ms + NORM_EPS) * ng_ref[...]).astype(bf16)

    def proj(c0):
        return jnp.dot(xn_ref[...], w_ref[:, c0:c0 + CHUNK], preferred_element_type=f32)

    n_perm_used = [0]

    def store_head_pair(group_refs, hp, val):
        g, h = divmod(hp, N_HP_GROUP)
        d = DILATIONS[g]
        if d == 1:
            group_refs[g][h] = val.astype(bf16)
            return
        buf = perm_ref.at[n_perm_used[0] % N_PERM]
        n_perm_used[0] += 1
        pitch = _perm_pitch(d)
        if pitch == d:
            buf[0:TM_IN, :] = val
        else:
            for j in range(TM_IN // d):
                buf[j * pitch:j * pitch + d, :] = val[j * d:(j + 1) * d, :]
        for r in range(d):
            group_refs[g][h, :, r * HP_WIDTH:(r + 1) * HP_WIDTH] = (
                buf[pl.ds(r, TM_IN // d, stride=pitch), :].astype(bf16))

    row = lax.broadcasted_iota(jnp.int32, (F32_ROWS, 1), 0)
    edge_ref[EDGE_ROWS:, :] = jnp.zeros((F32_ROWS - EDGE_ROWS, W_CONV), f32)
    for j in range(0, W_CONV, CHUNK):
        cols = slice(j, j + CHUNK)
        u = proj(OFF_C + j) * proj(OFF_H + j)
        z = proj(OFF_ZC + j)
        a = proj(OFF_B + j) * z * _sigmoid(z)
        rolled = pltpu.roll(u, 1, axis=0)
        u_prev = jnp.concatenate([jnp.where(row == 0, 0.0, rolled[:F32_ROWS]), rolled[F32_ROWS:]], axis=0)
        rolled = pltpu.roll(u, TM_IN - 1, axis=0)
        u_next = jnp.concatenate([rolled[:-F32_ROWS],
                                  jnp.where(row == F32_ROWS - 1, 0.0, rolled[-F32_ROWS:])], axis=0)
        conv = (u_prev * cw_ref[0:1, cols] + u * cw_ref[1:2, cols] + u_next * cw_ref[2:3, cols]
                + cb_ref[:, cols])
        yc_ref[:, cols] = (a * conv).astype(bf16)
        edge_ref[0:1, cols] = u[0:1]
        edge_ref[1:2, cols] = u[TM_IN - 1:TM_IN]
        edge_ref[2:3, cols] = a[0:1]
        edge_ref[3:4, cols] = a[TM_IN - 1:TM_IN]

    lane = lax.broadcasted_iota(jnp.int32, (1, HP_WIDTH), 1)
    first_half = (lane % HEAD_DIM) < (HEAD_DIM // 2)

    def qk_epilogue(t, cos_ref, sin_ref, refs, c):
        ss = jnp.dot((t * t).astype(bf16), ones_ref[...], preferred_element_type=f32)
        r = lax.rsqrt(ss * (1.0 / HEAD_DIM) + NORM_EPS)
        for p in range(HP_PER_CHUNK):
            cols = slice(p * HP_WIDTH, (p + 1) * HP_WIDTH)
            th = t[:, cols]
            partner = jnp.where(first_half,
                                pltpu.roll(th, HP_WIDTH - HEAD_DIM // 2, axis=1),
                                pltpu.roll(th, HEAD_DIM // 2, axis=1))
            store_head_pair(refs, c * HP_PER_CHUNK + p,
                            (th * cos_ref[...] + partner * sin_ref[...]) * r[:, cols])

    qk_chunks = [(OFF_Q + c * CHUNK, cq_ref, sq_ref, (q0_ref, q1_ref, q2_ref), c)
                 for c in range(W_QKV // CHUNK)]
    qk_chunks += [(OFF_K + c * CHUNK, ck_ref, sk_ref, (k0_ref, k1_ref, k2_ref), c)
                  for c in range(W_QKV // CHUNK)]
    for i, ch in enumerate(qk_chunks):
        t_ref[i % 2] = proj(ch[0])
        if i > 0:
            qk_epilogue(t_ref[(i - 1) % 2], *qk_chunks[i - 1][1:])
    qk_epilogue(t_ref[(len(qk_chunks) - 1) % 2], *qk_chunks[-1][1:])

    for c in range(W_QKV // CHUNK):
        t = proj(OFF_V + c * CHUNK)
        for p in range(HP_PER_CHUNK):
            store_head_pair((v0_ref, v1_ref, v2_ref), c * HP_PER_CHUNK + p,
                            t[:, p * HP_WIDTH:(p + 1) * HP_WIDTH])

    for j in range(0, W_ATTN_OUT, CHUNK):
        z = proj(OFF_ZA + j)
        sza_ref[:, j:j + CHUNK] = (z * _sigmoid(z)).astype(bf16)
    for j in range(0, D_MODEL, CHUNK):
        sgc_ref[:, j:j + CHUNK] = _sigmoid(proj(OFF_GC + j)).astype(bf16)
        sga_ref[:, j:j + CHUNK] = _sigmoid(proj(OFF_GA + j)).astype(bf16)


def _in_proj(x, norm_g, w_in, tables, ones_bd, conv_w, conv_b):
    B, S, _ = x.shape
    nt = S // TM_IN
    const = lambda b, t: (0, 0)
    tok = lambda b, t: (b, t, 0)
    sq = pl.Squeezed()
    tok_spec = lambda w: pl.BlockSpec((sq, TM_IN, w), tok)
    tok_shape = lambda w: jax.ShapeDtypeStruct((B, S, w), bf16)
    grp_shapes = tuple(jax.ShapeDtypeStruct((B, N_HP_GROUP, S // d, d * HP_WIDTH), bf16)
                       for d in DILATIONS)
    grp_specs = tuple(pl.BlockSpec((sq, N_HP_GROUP, TM_IN // d, d * HP_WIDTH),
                                   lambda b, t: (b, 0, t, 0)) for d in DILATIONS)
    return pl.pallas_call(
        _in_proj_body,
        grid=(B, nt),
        in_specs=[
            tok_spec(D_MODEL),
            pl.BlockSpec((1, D_MODEL), const),
            pl.BlockSpec((D_MODEL, IN_WIDTH), const, pipeline_mode=pl.Buffered(1)),
        ] + [pl.BlockSpec((TM_IN, HP_WIDTH), lambda b, t: (t, 0)) for _ in range(4)] + [
            pl.BlockSpec((CHUNK, CHUNK), const),
            pl.BlockSpec((CONV_TAPS, W_CONV), const),
            pl.BlockSpec((1, W_CONV), const),
        ],
        out_specs=(tok_spec(W_CONV), pl.BlockSpec((sq, F32_ROWS, W_CONV), tok)) + grp_specs * 3
                  + (tok_spec(W_ATTN_OUT), tok_spec(D_MODEL), tok_spec(D_MODEL)),
        out_shape=(tok_shape(W_CONV), jax.ShapeDtypeStruct((B, nt * F32_ROWS, W_CONV), f32)) + grp_shapes * 3
                  + (tok_shape(W_ATTN_OUT), tok_shape(D_MODEL), tok_shape(D_MODEL)),
        scratch_shapes=[pltpu.VMEM((TM_IN, D_MODEL), bf16),
                        pltpu.VMEM((N_PERM, PERM_ROWS, HP_WIDTH), f32),
                        pltpu.VMEM((2, TM_IN, CHUNK), f32)],
        compiler_params=pltpu.CompilerParams(
            dimension_semantics=("arbitrary", "arbitrary"),
            vmem_limit_bytes=VMEM_LIMIT_IN),
        name="in_proj",
    )(x, norm_g, w_in, *tables, ones_bd, conv_w, conv_b)


def _scores(q, ks):
    lane = lax.broadcasted_iota(jnp.int32, (BQ, HP_WIDTH), 1)
    head0 = lane < HEAD_DIM
    zero = jnp.zeros_like(q)
    q2 = jnp.concatenate([jnp.where(head0, q, zero), jnp.where(head0, zero, q)], axis=0)
    return lax.dot_general(q2, ks, (((1,), (1,)), ((), ())), preferred_element_type=f32)


def _softmax_pv(s, vs, bias):
    lane = lax.broadcasted_iota(jnp.int32, (BQ, HP_WIDTH), 1)
    head0 = lane < HEAD_DIM
    s = s + bias
    m = jnp.max(s, axis=1, keepdims=True)
    p = jnp.exp2(s - m).astype(bf16)
    vs1 = jnp.concatenate([vs, jnp.ones_like(vs)], axis=1)
    pv = jnp.dot(p, vs1, preferred_element_type=f32)
    acc = jnp.where(head0, pv[:BQ, :HP_WIDTH], pv[BQ:, :HP_WIDTH])
    l_sel = jnp.where(head0, pv[:BQ, HP_WIDTH:], pv[BQ:, HP_WIDTH:])
    m_sel = jnp.where(head0, m[:BQ], m[BQ:])
    return acc, m_sel, l_sel


def _band_attn_body(bias_ref, q0_ref, k0_ref, v0_ref, q1_ref, k1_ref, v1_ref,
                    q2_ref, k2_ref, v2_ref, o_ref,
                    o0_s, m0_s, l0_s, o1_s, m1_s, l1_s, o2_s, m2_s, l2_s, s_scr):
    def addr(n, L):
        if isinstance(n, int):
            qs = n * BQ
            st = min(max(qs - BAND_HALF, 0), L - SPAN)
            return qs, st, (qs - st) // BAND_HALF
        qs = pl.multiple_of(n * BQ, BQ)
        st = pl.multiple_of(jnp.clip(n * BQ - BAND_HALF, 0, L - SPAN), BAND_HALF)
        return qs, st, lax.shift_right_logical(qs - st, BAND_HALF.bit_length() - 1)

    def scores_stage(blocks, slot):
        for j, (q_ref, k_ref, _, cols, n, L, _) in enumerate(blocks):
            qs, st, _ = addr(n, L)
            s_scr[slot, j] = _scores(q_ref[pl.ds(qs, BQ), cols], k_ref[pl.ds(st, SPAN), cols])

    def softmax_stage(blocks, slot, store):
        for j, (_, _, v_ref, cols, n, L, r) in enumerate(blocks):
            qs, st, case = addr(n, L)
            acc, m, l = _softmax_pv(s_scr[slot, j], v_ref[pl.ds(st, SPAN), cols], bias_ref[case])
            store(qs, acc, m, l, r)

    def run_group(trips, blocks_of, store, next_blocks):
        assert trips % STEP_ITERS == 0 and STEP_ITERS % 2 == 0

        def iters(i, last):
            for k in range(STEP_ITERS):
                nxt = blocks_of(i + k + 1) if not (last and k == STEP_ITERS - 1) else next_blocks
                if nxt is not None:
                    scores_stage(nxt, (k + 1) % 2)
                softmax_stage(blocks_of(i + k), k % 2, store)

        def step(h, carry):
            iters(STEP_ITERS * h, False)
            return carry

        lax.fori_loop(0, trips // STEP_ITERS - 1, step, 0)
        iters(trips - STEP_ITERS, True)

    d_last = DILATIONS[-1]
    pitch = o_ref.shape[0] // d_last + RES_PAD

    def make_store(o_s, m_s, l_s, d):
        def store(qs, acc, m, l, r):
            if d == 1:
                rows = pl.ds(qs, BQ)
            elif d == d_last:
                rows = pl.ds(pl.multiple_of(r * pitch + qs, 8), BQ)
            else:
                rows = pl.ds(qs * d + r, BQ, stride=d)
            o_s[rows, :] = acc
            m_s[rows, :] = m
            l_s[rows, :] = l
        return store

    def blocks_of_group(q_ref, k_ref, v_ref, d):
        L = q_ref.shape[0]
        nb = L // BQ
        if d == 1:
            return nb // PIPE_BLOCKS, lambda i: [
                (q_ref, k_ref, v_ref, slice(None), PIPE_BLOCKS * i + j, L, 0) for j in range(PIPE_BLOCKS)]
        if d <= PIPE_BLOCKS:
            return nb, lambda i: [
                (q_ref, k_ref, v_ref, slice(r * HP_WIDTH, (r + 1) * HP_WIDTH), i, L, r) for r in range(d)]
        per_iter = PIPE_BLOCKS // nb

        def blocks(i):
            out = []
            for jr in range(per_iter):
                r = i * per_iter + jr
                cols = pl.ds(pl.multiple_of(r * HP_WIDTH, HP_WIDTH), HP_WIDTH)
                out += [(q_ref, k_ref, v_ref, cols, n, L, r) for n in range(nb)]
            return out
        return d // per_iter, blocks

    groups = []
    for (q_ref, k_ref, v_ref), o_s, m_s, l_s, d in (
            ((q0_ref, k0_ref, v0_ref), o0_s, m0_s, l0_s, DILATIONS[0]),
            ((q1_ref, k1_ref, v1_ref), o1_s, m1_s, l1_s, DILATIONS[1]),
            ((q2_ref, k2_ref, v2_ref), o2_s, m2_s, l2_s, DILATIONS[2])):
        trips, blocks = blocks_of_group(q_ref, k_ref, v_ref, d)
        groups.append((trips, blocks, make_store(o_s, m_s, l_s, d)))
    scores_stage(groups[0][1](0), 0)
    for g, (trips, blocks, store) in enumerate(groups):
        run_group(trips, blocks, store, groups[g + 1][1](0) if g + 1 < len(groups) else None)

    def merge(c, carry):
        rows = pl.ds(pl.multiple_of(c * BQ, BQ), BQ)

        def last_group(ref):
            base = c * (BQ // d_last)
            return jnp.concatenate([ref[pl.ds(base + a, d_last, stride=pitch), :]
                                    for a in range(BQ // d_last)], axis=0)

        m0, m1, m2 = m0_s[rows, :], m1_s[rows, :], last_group(m2_s)
        mx = jnp.maximum(jnp.maximum(m0, m1), m2)
        w0, w1, w2 = jnp.exp2(m0 - mx), jnp.exp2(m1 - mx), jnp.exp2(m2 - mx)
        num = w0 * o0_s[rows, :] + w1 * o1_s[rows, :] + w2 * last_group(o2_s)
        den = w0 * l0_s[rows, :] + w1 * l1_s[rows, :] + w2 * last_group(l2_s)
        o_ref[rows, :] = (num * (1.0 / den)).astype(bf16)
        return carry

    lax.fori_loop(0, o_ref.shape[0] // BQ, merge, 0, unroll=4)


def _band_bias():
    qi = np.arange(2 * BQ)[:, None] % BQ
    kj = np.arange(SPAN)[None, :]
    rel = np.stack([kj - qi - c * BAND_HALF for c in range(N_BIAS)])
    return jnp.asarray(np.where(np.abs(rel) <= BAND_HALF, 0.0, NEG_BIG), dtype=f32)


def _band_attn(qkv):
    B = qkv[0].shape[0]
    S = qkv[0].shape[2]
    sq = pl.Squeezed()
    hp_block = lambda b, h: (b, h, 0, 0)
    operands, specs = [], []
    for g, d in enumerate(DILATIONS):
        for a in range(3):
            operands.append(qkv[a * N_GROUPS + g])
            specs.append(pl.BlockSpec((sq, sq, S // d, d * HP_WIDTH), hp_block))
    return pl.pallas_call(
        _band_attn_body,
        grid=(B, N_HP_GROUP),
        in_specs=[pl.BlockSpec((N_BIAS, 2 * BQ, SPAN), lambda b, h: (0, 0, 0))] + specs,
        out_specs=pl.BlockSpec((sq, sq, S, HP_WIDTH), hp_block),
        out_shape=jax.ShapeDtypeStruct((B, N_HP_GROUP, S, HP_WIDTH), bf16),
        scratch_shapes=[pltpu.VMEM((S, HP_WIDTH), f32) for _ in range(3 * (N_GROUPS - 1))]
                       + [pltpu.VMEM((S + DILATIONS[-1] * RES_PAD, HP_WIDTH), f32) for _ in range(3)]
                       + [pltpu.VMEM((2, PIPE_BLOCKS, 2 * BQ, SPAN), f32)],
        compiler_params=pltpu.CompilerParams(
            dimension_semantics=("arbitrary", "arbitrary"),
            vmem_limit_bytes=VMEM_LIMIT_ATTN),
        name="band_attn",
    )(_band_bias(), *operands)


def _out_proj_body(x_ref, yc_ref, ep_ref, ec_ref, en_ref, o_ref, sza_ref, sgc_ref, sga_ref,
                   cw_ref, wbc_ref, wba_ref, wo_ref, out_ref):
    t = pl.program_id(1)
    nt = pl.num_programs(1)
    c_first = ec_ref[2:3, :] * cw_ref[0:1, :] * ep_ref[1:2, :] * jnp.where(t > 0, 1.0, 0.0)
    c_last = ec_ref[3:4, :] * cw_ref[2:3, :] * en_ref[0:1, :] * jnp.where(t < nt - 1, 1.0, 0.0)
    corr = jnp.concatenate([c_first, c_last, jnp.zeros((BF16_ROWS - 2, W_CONV), f32)], axis=0)
    y_ext = jnp.concatenate([yc_ref[...], corr.astype(bf16)], axis=0)
    p_ext = jnp.dot(y_ext, wbc_ref[...], preferred_element_type=f32)
    row = lax.broadcasted_iota(jnp.int32, (F32_ROWS, 1), 0)
    first = p_ext[:F32_ROWS] + jnp.where(row == 0, p_ext[TM_OUT:TM_OUT + 1], 0.0)
    last = (p_ext[TM_OUT - F32_ROWS:TM_OUT]
            + jnp.where(row == F32_ROWS - 1, p_ext[TM_OUT + 1:TM_OUT + 2], 0.0))
    p_c = jnp.concatenate([first, p_ext[F32_ROWS:TM_OUT - F32_ROWS], last], axis=0)
    o_cat = jnp.concatenate([o_ref[h] for h in range(N_HP_GROUP)], axis=1)
    p_a = jnp.dot(o_cat * sza_ref[...], wba_ref[...], preferred_element_type=f32)
    m = (sgc_ref[...].astype(f32) * p_c + sga_ref[...].astype(f32) * p_a).astype(bf16)
    out_ref[...] = x_ref[...] + jnp.dot(m, wo_ref[...], preferred_element_type=f32)


def _out_proj(x, yc, edge, o, sza, sgc, sga, conv_w, wbc, wba, wo):
    B, S, _ = x.shape
    nt = S // TM_OUT
    sq = pl.Squeezed()
    const = lambda b, t: (0, 0)
    tok = lambda b, t: (b, t, 0)
    tok_spec = lambda w: pl.BlockSpec((sq, TM_OUT, w), tok)
    edge_spec = lambda f: pl.BlockSpec((sq, F32_ROWS, W_CONV), f)
    return pl.pallas_call(
        _out_proj_body,
        grid=(B, nt),
        in_specs=[
            tok_spec(D_MODEL), tok_spec(W_CONV),
            edge_spec(lambda b, t: (b, jnp.maximum(t - 1, 0), 0)), edge_spec(tok),
            edge_spec(lambda b, t: (b, jnp.minimum(t + 1, nt - 1), 0)),
            pl.BlockSpec((sq, N_HP_GROUP, TM_OUT, HP_WIDTH), lambda b, t: (b, 0, t, 0)),
            tok_spec(W_ATTN_OUT), tok_spec(D_MODEL), tok_spec(D_MODEL),
            pl.BlockSpec((CONV_TAPS, W_CONV), const),
            pl.BlockSpec((W_CONV, D_MODEL), const), pl.BlockSpec((W_ATTN_OUT, D_MODEL), const),
            pl.BlockSpec((D_MODEL, D_MODEL), const),
        ],
        out_specs=tok_spec(D_MODEL),
        out_shape=jax.ShapeDtypeStruct((B, S, D_MODEL), f32),
        compiler_params=pltpu.CompilerParams(
            dimension_semantics=("arbitrary", "arbitrary"),
            vmem_limit_bytes=VMEM_LIMIT_OUT),
        name="out_proj",
    )(x, yc, edge, edge, edge, o, sza, sgc, sga, conv_w, wbc, wba, wo)


def _rotary_tables(S, q_gain, k_gain):
    half = HEAD_DIM // 2
    inv_freq = ROPE_THETA ** (-jnp.arange(0, half, dtype=f32) / half)
    ang = jnp.arange(S, dtype=f32)[:, None] * inv_freq[None, :]
    cos, sin = jnp.cos(ang), jnp.sin(ang)
    reps = HP_WIDTH // HEAD_DIM
    cos_t = jnp.tile(jnp.concatenate([cos, cos], axis=1), (1, reps))
    sin_t = jnp.tile(jnp.concatenate([-sin, sin], axis=1), (1, reps))
    tables = []
    for g, scale in ((q_gain, LOG2E / math.sqrt(HEAD_DIM)), (k_gain, 1.0)):
        g = g.astype(f32) * scale
        g_swap = jnp.concatenate([g[half:], g[:half]])
        tables += [cos_t * jnp.tile(g, reps)[None, :], sin_t * jnp.tile(g_swap, reps)[None, :]]
    return tables


def kernel(x, norm_g, w_in, conv_w, conv_b, q_norm_g, k_norm_g, w_branch_conv, w_branch_attn, w_out):
    B, S, D = x.shape
    assert D == D_MODEL and w_in.shape == (D_MODEL, IN_WIDTH)
    assert S % TM_IN == 0 and TM_OUT == TM_IN and (S // DILATIONS[-1]) % BQ == 0
    assert (TM_IN // DILATIONS[-1]) % BF16_ROWS == 0 and (S // BQ) % PIPE_BLOCKS == 0
    assert conv_w.shape == (CONV_TAPS, W_CONV)
    head_id = np.arange(CHUNK) // HEAD_DIM
    ones_bd = jnp.asarray(head_id[:, None] == head_id[None, :], dtype=bf16)
    conv_w = conv_w.astype(f32)

    outs = _in_proj(x, norm_g.astype(f32)[None, :], w_in.astype(bf16),
                    _rotary_tables(S, q_norm_g, k_norm_g), ones_bd, conv_w, conv_b.astype(f32)[None, :])
    yc, edge = outs[0], outs[1]
    sza, sgc, sga = outs[11], outs[12], outs[13]
    o = _band_attn(outs[2:11])
    return _out_proj(x, yc, edge, o, sza, sgc, sga, conv_w,
                     w_branch_conv.astype(bf16), w_branch_attn.astype(bf16), w_out.astype(bf16))
```

```python
import math

import jax
import jax.numpy as jnp
import numpy as np
from jax import lax
from jax.experimental import pallas as pl
from jax.experimental.pallas import tpu as pltpu

D_MODEL = 1024
W_CONV = 1024
HEAD_DIM = 64
HEADS_PER_GROUP = 8
DILATIONS = (1, 4, 16)
BAND_HALF = 64
N_GROUPS = 3
W_QKV = N_GROUPS * HEADS_PER_GROUP * HEAD_DIM
W_ATTN_OUT = HEADS_PER_GROUP * HEAD_DIM
ROPE_THETA = 10000.0
NORM_EPS = 1e-6

OFF_B, OFF_C, OFF_H, OFF_ZC = 0, 1024, 2048, 3072
OFF_Q = 4096
OFF_K = OFF_Q + W_QKV
OFF_V = OFF_K + W_QKV
OFF_ZA = OFF_V + W_QKV
OFF_GC = OFF_ZA + W_ATTN_OUT
OFF_GA = OFF_GC + D_MODEL
IN_WIDTH = OFF_GA + D_MODEL

LANES = 128
MXU_WIDTH = 256
HP_WIDTH = 2 * HEAD_DIM
N_HP_GROUP = HEADS_PER_GROUP // 2
CONV_TAPS = 3
EDGE_ROWS = 4
F32_ROWS = 8
BF16_ROWS = 16

TM_IN = 512
TM_OUT = 1024
SUB_TILES = TM_OUT // TM_IN
CHUNK = MXU_WIDTH
HP_PER_CHUNK = CHUNK // HP_WIDTH
PERM_FREE_STRIDE = 4
N_PERM = 4
BQ = 128
SPAN = BQ + 2 * BAND_HALF
RES_PAD = 8
STEP_ITERS = 4
PIPE_BLOCKS = 4
N_BIAS = SPAN // BAND_HALF - 1
NEG_BIG = -1e30
LOG2E = math.log2(math.e)

VMEM_LIMIT_IN = 58 * 1024 * 1024
VMEM_LIMIT_ATTN = 48 * 1024 * 1024
VMEM_LIMIT_OUT = 56 * 1024 * 1024

f32 = jnp.float32
bf16 = jnp.bfloat16


def _perm_pitch(d):
    return d if d <= PERM_FREE_STRIDE else d + F32_ROWS


PERM_ROWS = max(TM_IN // d * _perm_pitch(d) for d in DILATIONS)


def _sigmoid(z):
    return 1.0 / (1.0 + jnp.exp(-z))


def _in_proj_body(x_ref, ng_ref, w_ref, cq_ref, sq_ref, ck_ref, sk_ref, ones_ref, cw_ref, cb_ref,
                  yc_ref, edge_ref, q0_ref, q1_ref, q2_ref, k0_ref, k1_ref, k2_ref,
                  v0_ref, v1_ref, v2_ref, sza_ref, sgc_ref, sga_ref, xn_ref, perm_ref, t_ref):
    x = x_ref[...]
    ms = jnp.mean(x * x, axis=-1, keepdims=True)
    xn_ref[...] = (x * lax.rsqrt(ms + NORM_EPS) * ng_ref[...]).astype(bf16)

    def proj(c0):
        return jnp.dot(xn_ref[...], w_ref[:, c0:c0 + CHUNK], preferred_element_type=f32)

    n_perm_used = [0]

    def store_head_pair(group_refs, hp, val):
        g, h = divmod(hp, N_HP_GROUP)
        d = DILATIONS[g]
        if d == 1:
            group_refs[g][h] = val.astype(bf16)
            return
        buf = perm_ref.at[n_perm_used[0] % N_PERM]
        n_perm_used[0] += 1
        pitch = _perm_pitch(d)
        if pitch == d:
            buf[0:TM_IN, :] = val
        else:
            for j in range(TM_IN // d):
                buf[j * pitch:j * pitch + d, :] = val[j * d:(j + 1) * d, :]
        for r in range(d):
            group_refs[g][h, :, r * HP_WIDTH:(r + 1) * HP_WIDTH] = (
                buf[pl.ds(r, TM_IN // d, stride=pitch), :].astype(bf16))

    row = lax.broadcasted_iota(jnp.int32, (F32_ROWS, 1), 0)
    edge_ref[EDGE_ROWS:, :] = jnp.zeros((F32_ROWS - EDGE_ROWS, W_CONV), f32)
    for j in range(0, W_CONV, CHUNK):
        cols = slice(j, j + CHUNK)
        u = proj(OFF_C + j) * proj(OFF_H + j)
        z = proj(OFF_ZC + j)
        a = proj(OFF_B + j) * z * _sigmoid(z)
        rolled = pltpu.roll(u, 1, axis=0)
        u_prev = jnp.concatenate([jnp.where(row == 0, 0.0, rolled[:F32_ROWS]), rolled[F32_ROWS:]], axis=0)
        rolled = pltpu.roll(u, TM_IN - 1, axis=0)
        u_next = jnp.concatenate([rolled[:-F32_ROWS],
                                  jnp.where(row == F32_ROWS - 1, 0.0, rolled[-F32_ROWS:])], axis=0)
        conv = (u_prev * cw_ref[0:1, cols] + u * cw_ref[1:2, cols] + u_next * cw_ref[2:3, cols]
                + cb_ref[:, cols])
        yc_ref[:, cols] = (a * conv).astype(bf16)
        edge_ref[0:1, cols] = u[0:1]
        edge_ref[1:2, cols] = u[TM_IN - 1:TM_IN]
        edge_ref[2:3, cols] = a[0:1]
        edge_ref[3:4, cols] = a[TM_IN - 1:TM_IN]

    lane = lax.broadcasted_iota(jnp.int32, (1, HP_WIDTH), 1)
    first_half = (lane % HEAD_DIM) < (HEAD_DIM // 2)

    def qk_epilogue(t, cos_ref, sin_ref, refs, c):
        ss = jnp.dot((t * t).astype(bf16), ones_ref[...], preferred_element_type=f32)
        r = lax.rsqrt(ss * (1.0 / HEAD_DIM) + NORM_EPS)
        for p in range(HP_PER_CHUNK):
            cols = slice(p * HP_WIDTH, (p + 1) * HP_WIDTH)
            th = t[:, cols]
            partner = jnp.where(first_half,
                                pltpu.roll(th, HP_WIDTH - HEAD_DIM // 2, axis=1),
                                pltpu.roll(th, HEAD_DIM // 2, axis=1))
            store_head_pair(refs, c * HP_PER_CHUNK + p,
                            (th * cos_ref[...] + partner * sin_ref[...]) * r[:, cols])

    qk_chunks = [(OFF_Q + c * CHUNK, cq_ref, sq_ref, (q0_ref, q1_ref, q2_ref), c)
                 for c in range(W_QKV // CHUNK)]
    qk_chunks += [(OFF_K + c * CHUNK, ck_ref, sk_ref, (k0_ref, k1_ref, k2_ref), c)
                  for c in range(W_QKV // CHUNK)]
    for i, ch in enumerate(qk_chunks):
        t_ref[i % 2] = proj(ch[0])
        if i > 0:
            qk_epilogue(t_ref[(i - 1) % 2], *qk_chunks[i - 1][1:])
    qk_epilogue(t_ref[(len(qk_chunks) - 1) % 2], *qk_chunks[-1][1:])

    for c in range(W_QKV // CHUNK):
        t = proj(OFF_V + c * CHUNK)
        for p in range(HP_PER_CHUNK):
            store_head_pair((v0_ref, v1_ref, v2_ref), c * HP_PER_CHUNK + p,
                            t[:, p * HP_WIDTH:(p + 1) * HP_WIDTH])

    for j in range(0, W_ATTN_OUT, CHUNK):
        z = proj(OFF_ZA + j)
        sza_ref[:, j:j + CHUNK] = (z * _sigmoid(z)).astype(bf16)
    for j in range(0, D_MODEL, CHUNK):
        sgc_ref[:, j:j + CHUNK] = _sigmoid(proj(OFF_GC + j)).astype(bf16)
        sga_ref[:, j:j + CHUNK] = _sigmoid(proj(OFF_GA + j)).astype(bf16)


def _in_proj(x, norm_g, w_in, tables, ones_bd, conv_w, conv_b):
    B, S, _ = x.shape
    nt = S // TM_IN
    const = lambda b, t: (0, 0)
    tok = lambda b, t: (b, t, 0)
    sq = pl.Squeezed()
    tok_spec = lambda w: pl.BlockSpec((sq, TM_IN, w), tok)
    tok_shape = lambda w: jax.ShapeDtypeStruct((B, S, w), bf16)
    grp_shapes = tuple(jax.ShapeDtypeStruct((B, N_HP_GROUP, S // d, d * HP_WIDTH), bf16)
                       for d in DILATIONS)
    grp_specs = tuple(pl.BlockSpec((sq, N_HP_GROUP, TM_IN // d, d * HP_WIDTH),
                                   lambda b, t: (b, 0, t, 0)) for d in DILATIONS)
    return pl.pallas_call(
        _in_proj_body,
        grid=(B, nt),
        in_specs=[
            tok_spec(D_MODEL),
            pl.BlockSpec((1, D_MODEL), const),
            pl.BlockSpec((D_MODEL, IN_WIDTH), const, pipeline_mode=pl.Buffered(1)),
        ] + [pl.BlockSpec((TM_IN, HP_WIDTH), lambda b, t: (t, 0)) for _ in range(4)] + [
            pl.BlockSpec((CHUNK, CHUNK), const),
            pl.BlockSpec((CONV_TAPS, W_CONV), const),
            pl.BlockSpec((1, W_CONV), const),
        ],
        out_specs=(tok_spec(W_CONV), pl.BlockSpec((sq, F32_ROWS, W_CONV), tok)) + grp_specs * 3
                  + (tok_spec(W_ATTN_OUT), tok_spec(D_MODEL), tok_spec(D_MODEL)),
        out_shape=(tok_shape(W_CONV), jax.ShapeDtypeStruct((B, nt * F32_ROWS, W_CONV), f32)) + grp_shapes * 3
                  + (tok_shape(W_ATTN_OUT), tok_shape(D_MODEL), tok_shape(D_MODEL)),
        scratch_shapes=[pltpu.VMEM((TM_IN, D_MODEL), bf16),
                        pltpu.VMEM((N_PERM, PERM_ROWS, HP_WIDTH), f32),
                        pltpu.VMEM((2, TM_IN, CHUNK), f32)],
        compiler_params=pltpu.CompilerParams(
            dimension_semantics=("arbitrary", "arbitrary"),
            vmem_limit_bytes=VMEM_LIMIT_IN),
        name="in_proj",
    )(x, norm_g, w_in, *tables, ones_bd, conv_w, conv_b)


def _scores(q, ks):
    lane = lax.broadcasted_iota(jnp.int32, (BQ, HP_WIDTH), 1)
    head0 = lane < HEAD_DIM
    zero = jnp.zeros_like(q)
    q2 = jnp.concatenate([jnp.where(head0, q, zero), jnp.where(head0, zero, q)], axis=0)
    return lax.dot_general(q2, ks, (((1,), (1,)), ((), ())), preferred_element_type=f32)


def _softmax_pv(s, vs, bias):
    lane = lax.broadcasted_iota(jnp.int32, (BQ, HP_WIDTH), 1)
    head0 = lane < HEAD_DIM
    s = s + bias
    m = jnp.max(s, axis=1, keepdims=True)
    p = jnp.exp2(s - m).astype(bf16)
    vs1 = jnp.concatenate([vs, jnp.ones_like(vs)], axis=1)
    pv = jnp.dot(p, vs1, preferred_element_type=f32)
    acc = jnp.where(head0, pv[:BQ, :HP_WIDTH], pv[BQ:, :HP_WIDTH])
    l_sel = jnp.where(head0, pv[:BQ, HP_WIDTH:], pv[BQ:, HP_WIDTH:])
    m_sel = jnp.where(head0, m[:BQ], m[BQ:])
    return acc, m_sel, l_sel


def _band_attn_body(bias_ref, q0_ref, k0_ref, v0_ref, q1_ref, k1_ref, v1_ref,
                    q2_ref, k2_ref, v2_ref, o_ref,
                    o0_s, m0_s, l0_s, o1_s, m1_s, l1_s, o2_s, m2_s, l2_s, s_scr):
    def addr(n, L):
        if isinstance(n, int):
            qs = n * BQ
            st = min(max(qs - BAND_HALF, 0), L - SPAN)
            return qs, st, (qs - st) // BAND_HALF
        qs = pl.multiple_of(n * BQ, BQ)
        st = pl.multiple_of(jnp.clip(n * BQ - BAND_HALF, 0, L - SPAN), BAND_HALF)
        return qs, st, lax.shift_right_logical(qs - st, BAND_HALF.bit_length() - 1)

    def scores_stage(blocks, slot):
        for j, (q_ref, k_ref, _, cols, n, L, _) in enumerate(blocks):
            qs, st, _ = addr(n, L)
            s_scr[slot, j] = _scores(q_ref[pl.ds(qs, BQ), cols], k_ref[pl.ds(st, SPAN), cols])

    def softmax_stage(blocks, slot, store):
        for j, (_, _, v_ref, cols, n, L, r) in enumerate(blocks):
            qs, st, case = addr(n, L)
            acc, m, l = _softmax_pv(s_scr[slot, j], v_ref[pl.ds(st, SPAN), cols], bias_ref[case])
            store(qs, acc, m, l, r)

    def run_group(trips, blocks_of, store, next_blocks):
        assert trips % STEP_ITERS == 0 and STEP_ITERS % 2 == 0

        def iters(i, last):
            for k in range(STEP_ITERS):
                nxt = blocks_of(i + k + 1) if not (last and k == STEP_ITERS - 1) else next_blocks
                if nxt is not None:
                    scores_stage(nxt, (k + 1) % 2)
                softmax_stage(blocks_of(i + k), k % 2, store)

        def step(h, carry):
            iters(STEP_ITERS * h, False)
            return carry

        lax.fori_loop(0, trips // STEP_ITERS - 1, step, 0)
        iters(trips - STEP_ITERS, True)

    d_last = DILATIONS[-1]
    pitch = o_ref.shape[0] // d_last + RES_PAD

    def make_store(o_s, m_s, l_s, d):
        def store(qs, acc, m, l, r):
            if d == 1:
                rows = pl.ds(qs, BQ)
            elif d == d_last:
                rows = pl.ds(pl.multiple_of(r * pitch + qs, 8), BQ)
            else:
                rows = pl.ds(qs * d + r, BQ, stride=d)
            o_s[rows, :] = acc
            m_s[rows, :] = m
            l_s[rows, :] = l
        return store

    def blocks_of_group(q_ref, k_ref, v_ref, d):
        L = q_ref.shape[0]
        nb = L // BQ
        if d == 1:
            return nb // PIPE_BLOCKS, lambda i: [
                (q_ref, k_ref, v_ref, slice(None), PIPE_BLOCKS * i + j, L, 0) for j in range(PIPE_BLOCKS)]
        if d <= PIPE_BLOCKS:
            return nb, lambda i: [
                (q_ref, k_ref, v_ref, slice(r * HP_WIDTH, (r + 1) * HP_WIDTH), i, L, r) for r in range(d)]
        per_iter = PIPE_BLOCKS // nb

        def blocks(i):
            out = []
            for jr in range(per_iter):
                r = i * per_iter + jr
                cols = pl.ds(pl.multiple_of(r * HP_WIDTH, HP_WIDTH), HP_WIDTH)
                out += [(q_ref, k_ref, v_ref, cols, n, L, r) for n in range(nb)]
            return out
        return d // per_iter, blocks

    groups = []
    for (q_ref, k_ref, v_ref), o_s, m_s, l_s, d in (
            ((q0_ref, k0_ref, v0_ref), o0_s, m0_s, l0_s, DILATIONS[0]),
            ((q1_ref, k1_ref, v1_ref), o1_s, m1_s, l1_s, DILATIONS[1]),
            ((q2_ref, k2_ref, v2_ref), o2_s, m2_s, l2_s, DILATIONS[2])):
        trips, blocks = blocks_of_group(q_ref, k_ref, v_ref, d)
        groups.append((trips, blocks, make_store(o_s, m_s, l_s, d)))
    scores_stage(groups[0][1](0), 0)
    for g, (trips, blocks, store) in enumerate(groups):
        run_group(trips, blocks, store, groups[g + 1][1](0) if g + 1 < len(groups) else None)

    def merge(c, carry):
        rows = pl.ds(pl.multiple_of(c * BQ, BQ), BQ)

        def last_group(ref):
            base = c * (BQ // d_last)
            return jnp.concatenate([ref[pl.ds(base + a, d_last, stride=pitch), :]
                                    for a in range(BQ // d_last)], axis=0)

        m0, m1, m2 = m0_s[rows, :], m1_s[rows, :], last_group(m2_s)
        mx = jnp.maximum(jnp.maximum(m0, m1), m2)
        w0, w1, w2 = jnp.exp2(m0 - mx), jnp.exp2(m1 - mx), jnp.exp2(m2 - mx)
        num = w0 * o0_s[rows, :] + w1 * o1_s[rows, :] + w2 * last_group(o2_s)
        den = w0 * l0_s[rows, :] + w1 * l1_s[rows, :] + w2 * last_group(l2_s)
        o_ref[rows, :] = (num * (1.0 / den)).astype(bf16)
        return carry

    lax.fori_loop(0, o_ref.shape[0] // BQ, merge, 0, unroll=4)


def _band_bias():
    qi = np.arange(2 * BQ)[:, None] % BQ
    kj = np.arange(SPAN)[None, :]
    rel = np.stack([kj - qi - c * BAND_HALF for c in range(N_BIAS)])
    return jnp.asarray(np.where(np.abs(rel) <= BAND_HALF, 0.0, NEG_BIG), dtype=f32)


def _band_attn(qkv):
    B = qkv[0].shape[0]
    S = qkv[0].shape[2]
    sq = pl.Squeezed()
    hp_block = lambda b, h: (b, h, 0, 0)
    operands, specs = [], []
    for g, d in enumerate(DILATIONS):
        for a in range(3):
            operands.append(qkv[a * N_GROUPS + g])
            specs.append(pl.BlockSpec((sq, sq, S // d, d * HP_WIDTH), hp_block))
    return pl.pallas_call(
        _band_attn_body,
        grid=(B, N_HP_GROUP),
        in_specs=[pl.BlockSpec((N_BIAS, 2 * BQ, SPAN), lambda b, h: (0, 0, 0))] + specs,
        out_specs=pl.BlockSpec((sq, sq, S, HP_WIDTH), hp_block),
        out_shape=jax.ShapeDtypeStruct((B, N_HP_GROUP, S, HP_WIDTH), bf16),
        scratch_shapes=[pltpu.VMEM((S, HP_WIDTH), f32) for _ in range(3 * (N_GROUPS - 1))]
                       + [pltpu.VMEM((S + DILATIONS[-1] * RES_PAD, HP_WIDTH), f32) for _ in range(3)]
                       + [pltpu.VMEM((2, PIPE_BLOCKS, 2 * BQ, SPAN), f32)],
        compiler_params=pltpu.CompilerParams(
            dimension_semantics=("arbitrary", "arbitrary"),
            vmem_limit_bytes=VMEM_LIMIT_ATTN),
        name="band_attn",
    )(_band_bias(), *operands)


def _out_proj_body(x_ref, yc_ref, ep_ref, ec_ref, en_ref, o_ref, sza_ref, sgc_ref, sga_ref,
                   cw_ref, wbc_ref, wba_ref, wo_ref, out_ref):
    t = pl.program_id(1)
    nt = pl.num_programs(1)

    def edge_row(ref, sub, k):
        return ref[sub * F32_ROWS + k:sub * F32_ROWS + k + 1, :]

    corr = []
    for i in range(SUB_TILES):
        u_before = (edge_row(ep_ref, SUB_TILES - 1, 1) * jnp.where(t > 0, 1.0, 0.0) if i == 0
                    else edge_row(ec_ref, i - 1, 1))
        u_after = (edge_row(en_ref, 0, 0) * jnp.where(t < nt - 1, 1.0, 0.0) if i == SUB_TILES - 1
                   else edge_row(ec_ref, i + 1, 0))
        corr += [edge_row(ec_ref, i, 2) * cw_ref[0:1, :] * u_before,
                 edge_row(ec_ref, i, 3) * cw_ref[2:3, :] * u_after]
    corr = jnp.concatenate(corr + [jnp.zeros((BF16_ROWS - len(corr), W_CONV), f32)], axis=0)
    y_ext = jnp.concatenate([yc_ref[...], corr.astype(bf16)], axis=0)
    p_ext = jnp.dot(y_ext, wbc_ref[...], preferred_element_type=f32)
    row = lax.broadcasted_iota(jnp.int32, (F32_ROWS, 1), 0)
    extra = p_ext[TM_OUT:TM_OUT + F32_ROWS]
    pieces = []
    for i in range(SUB_TILES):
        lo, hi = i * TM_IN, (i + 1) * TM_IN
        pieces += [p_ext[lo:lo + F32_ROWS] + jnp.where(row == 0, extra[2 * i:2 * i + 1], 0.0),
                   p_ext[lo + F32_ROWS:hi - F32_ROWS],
                   p_ext[hi - F32_ROWS:hi] + jnp.where(row == F32_ROWS - 1, extra[2 * i + 1:2 * i + 2], 0.0)]
    p_c = jnp.concatenate(pieces, axis=0)
    o_cat = jnp.concatenate([o_ref[h] for h in range(N_HP_GROUP)], axis=1)
    p_a = jnp.dot(o_cat * sza_ref[...], wba_ref[...], preferred_element_type=f32)
    m = (sgc_ref[...].astype(f32) * p_c + sga_ref[...].astype(f32) * p_a).astype(bf16)
    out_ref[...] = x_ref[...] + jnp.dot(m, wo_ref[...], preferred_element_type=f32)


def _out_proj(x, yc, edge, o, sza, sgc, sga, conv_w, wbc, wba, wo):
    B, S, _ = x.shape
    nt = S // TM_OUT
    sq = pl.Squeezed()
    const = lambda b, t: (0, 0)
    tok = lambda b, t: (b, t, 0)
    tok_spec = lambda w: pl.BlockSpec((sq, TM_OUT, w), tok)
    edge_spec = lambda f: pl.BlockSpec((sq, SUB_TILES * F32_ROWS, W_CONV), f)
    return pl.pallas_call(
        _out_proj_body,
        grid=(B, nt),
        in_specs=[
            tok_spec(D_MODEL), tok_spec(W_CONV),
            edge_spec(lambda b, t: (b, jnp.maximum(t - 1, 0), 0)), edge_spec(tok),
            edge_spec(lambda b, t: (b, jnp.minimum(t + 1, nt - 1), 0)),
            pl.BlockSpec((sq, N_HP_GROUP, TM_OUT, HP_WIDTH), lambda b, t: (b, 0, t, 0)),
            tok_spec(W_ATTN_OUT), tok_spec(D_MODEL), tok_spec(D_MODEL),
            pl.BlockSpec((CONV_TAPS, W_CONV), const),
            pl.BlockSpec((W_CONV, D_MODEL), const), pl.BlockSpec((W_ATTN_OUT, D_MODEL), const),
            pl.BlockSpec((D_MODEL, D_MODEL), const),
        ],
        out_specs=tok_spec(D_MODEL),
        out_shape=jax.ShapeDtypeStruct((B, S, D_MODEL), f32),
        compiler_params=pltpu.CompilerParams(
            dimension_semantics=("arbitrary", "arbitrary"),
            vmem_limit_bytes=VMEM_LIMIT_OUT),
        name="out_proj",
    )(x, yc, edge, edge, edge, o, sza, sgc, sga, conv_w, wbc, wba, wo)


def _rotary_tables(S, q_gain, k_gain):
    half = HEAD_DIM // 2
    inv_freq = ROPE_THETA ** (-jnp.arange(0, half, dtype=f32) / half)
    ang = jnp.arange(S, dtype=f32)[:, None] * inv_freq[None, :]
    cos, sin = jnp.cos(ang), jnp.sin(ang)
    reps = HP_WIDTH // HEAD_DIM
    cos_t = jnp.tile(jnp.concatenate([cos, cos], axis=1), (1, reps))
    sin_t = jnp.tile(jnp.concatenate([-sin, sin], axis=1), (1, reps))
    tables = []
    for g, scale in ((q_gain, LOG2E / math.sqrt(HEAD_DIM)), (k_gain, 1.0)):
        g = g.astype(f32) * scale
        g_swap = jnp.concatenate([g[half:], g[:half]])
        tables += [cos_t * jnp.tile(g, reps)[None, :], sin_t * jnp.tile(g_swap, reps)[None, :]]
    return tables


def kernel(x, norm_g, w_in, conv_w, conv_b, q_norm_g, k_norm_g, w_branch_conv, w_branch_attn, w_out):
    B, S, D = x.shape
    assert D == D_MODEL and w_in.shape == (D_MODEL, IN_WIDTH)
    assert S % TM_OUT == 0 and TM_OUT % TM_IN == 0 and 2 * SUB_TILES <= F32_ROWS and (S // DILATIONS[-1]) % BQ == 0
    assert (TM_IN // DILATIONS[-1]) % BF16_ROWS == 0 and (S // BQ) % PIPE_BLOCKS == 0
    assert conv_w.shape == (CONV_TAPS, W_CONV)
    head_id = np.arange(CHUNK) // HEAD_DIM
    ones_bd = jnp.asarray(head_id[:, None] == head_id[None, :], dtype=bf16)
    conv_w = conv_w.astype(f32)

    outs = _in_proj(x, norm_g.astype(f32)[None, :], w_in.astype(bf16),
                    _rotary_tables(S, q_norm_g, k_norm_g), ones_bd, conv_w, conv_b.astype(f32)[None, :])
    yc, edge = outs[0], outs[1]
    sza, sgc, sga = outs[11], outs[12], outs[13]
    o = _band_attn(outs[2:11])
    return _out_proj(x, yc, edge, o, sza, sgc, sga, conv_w,
                     w_branch_conv.astype(bf16), w_branch_attn.astype(bf16), w_out.astype(bf16))
```

```python
import math

import jax
import jax.numpy as jnp
import numpy as np
from jax import lax
from jax.experimental import pallas as pl
from jax.experimental.pallas import tpu as pltpu

D_MODEL = 1024
W_CONV = 1024
HEAD_DIM = 64
HEADS_PER_GROUP = 8
DILATIONS = (1, 4, 16)
BAND_HALF = 64
N_GROUPS = 3
W_QKV = N_GROUPS * HEADS_PER_GROUP * HEAD_DIM
W_ATTN_OUT = HEADS_PER_GROUP * HEAD_DIM
ROPE_THETA = 10000.0
NORM_EPS = 1e-6

OFF_B, OFF_C, OFF_H, OFF_ZC = (i * W_CONV for i in range(4))
OFF_Q = 4 * W_CONV
OFF_K = OFF_Q + W_QKV
OFF_V = OFF_K + W_QKV
OFF_ZA = OFF_V + W_QKV
OFF_GC = OFF_ZA + W_ATTN_OUT
OFF_GA = OFF_GC + D_MODEL
IN_WIDTH = OFF_GA + D_MODEL

MXU_WIDTH = 256
HP_WIDTH = 2 * HEAD_DIM
N_HP_GROUP = HEADS_PER_GROUP // 2
CONV_TAPS = 3
EDGE_ROWS = 4
F32_ROWS = 8
BF16_ROWS = 16

TM_IN = 512
TM_OUT = 1024
SUB_TILES = TM_OUT // TM_IN
CHUNK = MXU_WIDTH
HP_PER_CHUNK = CHUNK // HP_WIDTH
PERM_FREE_STRIDE = 4
N_PERM = 4
BQ = 128
SPAN = BQ + 2 * BAND_HALF
RES_PAD = 4
STEP_ITERS = 4
PIPE_BLOCKS = 4
N_BIAS = SPAN // BAND_HALF - 1
NEG_BIG = -1e30
LOG2E = math.log2(math.e)

VMEM_LIMIT_IN = 58 * 1024 * 1024
VMEM_LIMIT_ATTN = 48 * 1024 * 1024
VMEM_LIMIT_OUT = 56 * 1024 * 1024

f32 = jnp.float32
bf16 = jnp.bfloat16


def _perm_pitch(d):
    return d if d <= PERM_FREE_STRIDE else d + F32_ROWS


PERM_ROWS = max(TM_IN // d * _perm_pitch(d) for d in DILATIONS)


def _sigmoid(z):
    return 1.0 / (1.0 + jnp.exp(-z))


def _in_proj_body(x_ref, ng_ref, w_ref, cq_ref, sq_ref, ck_ref, sk_ref, ones_ref, cw_ref, cb_ref,
                  yc_ref, edge_ref, q0_ref, q1_ref, q2_ref, k0_ref, k1_ref, k2_ref,
                  v0_ref, v1_ref, v2_ref, sza_ref, sgc_ref, sga_ref, xn_ref, perm_ref, t_ref):
    x = x_ref[...]
    ms = jnp.mean(x * x, axis=-1, keepdims=True)
    xn_ref[...] = (x * lax.rsqrt(ms + NORM_EPS) * ng_ref[...]).astype(bf16)

    def proj(c0):
        return jnp.dot(xn_ref[...], w_ref[:, c0:c0 + CHUNK], preferred_element_type=f32)

    n_perm_used = [0]

    def store_head_pair(group_refs, hp, val):
        g, h = divmod(hp, N_HP_GROUP)
        d = DILATIONS[g]
        if d == 1:
            group_refs[g][h] = val.astype(bf16)
            return
        buf = perm_ref.at[n_perm_used[0] % N_PERM]
        n_perm_used[0] += 1
        pitch = _perm_pitch(d)
        if pitch == d:
            buf[0:TM_IN, :] = val
        else:
            for j in range(TM_IN // d):
                buf[j * pitch:j * pitch + d, :] = val[j * d:(j + 1) * d, :]
        for r in range(d):
            group_refs[g][h, :, r * HP_WIDTH:(r + 1) * HP_WIDTH] = (
                buf[pl.ds(r, TM_IN // d, stride=pitch), :].astype(bf16))

    row = lax.broadcasted_iota(jnp.int32, (F32_ROWS, 1), 0)
    edge_ref[EDGE_ROWS:, :] = jnp.zeros((F32_ROWS - EDGE_ROWS, W_CONV), f32)
    for j in range(0, W_CONV, CHUNK):
        cols = slice(j, j + CHUNK)
        u = proj(OFF_C + j) * proj(OFF_H + j)
        z = proj(OFF_ZC + j)
        a = proj(OFF_B + j) * z * _sigmoid(z)
        rolled = pltpu.roll(u, 1, axis=0)
        u_prev = jnp.concatenate([jnp.where(row == 0, 0.0, rolled[:F32_ROWS]), rolled[F32_ROWS:]], axis=0)
        rolled = pltpu.roll(u, TM_IN - 1, axis=0)
        u_next = jnp.concatenate([rolled[:-F32_ROWS],
                                  jnp.where(row == F32_ROWS - 1, 0.0, rolled[-F32_ROWS:])], axis=0)
        conv = (u_prev * cw_ref[0:1, cols] + u * cw_ref[1:2, cols] + u_next * cw_ref[2:3, cols]
                + cb_ref[:, cols])
        yc_ref[:, cols] = (a * conv).astype(bf16)
        edge_ref[0:1, cols] = u[0:1]
        edge_ref[1:2, cols] = u[TM_IN - 1:TM_IN]
        edge_ref[2:3, cols] = a[0:1]
        edge_ref[3:4, cols] = a[TM_IN - 1:TM_IN]

    lane = lax.broadcasted_iota(jnp.int32, (1, HP_WIDTH), 1)
    first_half = (lane % HEAD_DIM) < (HEAD_DIM // 2)

    def qk_epilogue(t, cos_ref, sin_ref, refs, c):
        ss = jnp.dot((t * t).astype(bf16), ones_ref[...], preferred_element_type=f32)
        r = lax.rsqrt(ss * (1.0 / HEAD_DIM) + NORM_EPS)
        for p in range(HP_PER_CHUNK):
            cols = slice(p * HP_WIDTH, (p + 1) * HP_WIDTH)
            th = t[:, cols]
            partner = jnp.where(first_half,
                                pltpu.roll(th, HP_WIDTH - HEAD_DIM // 2, axis=1),
                                pltpu.roll(th, HEAD_DIM // 2, axis=1))
            store_head_pair(refs, c * HP_PER_CHUNK + p,
                            (th * cos_ref[...] + partner * sin_ref[...]) * r[:, cols])

    qk_chunks = [(OFF_Q + c * CHUNK, cq_ref, sq_ref, (q0_ref, q1_ref, q2_ref), c)
                 for c in range(W_QKV // CHUNK)]
    qk_chunks += [(OFF_K + c * CHUNK, ck_ref, sk_ref, (k0_ref, k1_ref, k2_ref), c)
                  for c in range(W_QKV // CHUNK)]
    for i, ch in enumerate(qk_chunks):
        t_ref[i % 2] = proj(ch[0])
        if i > 0:
            qk_epilogue(t_ref[(i - 1) % 2], *qk_chunks[i - 1][1:])
    qk_epilogue(t_ref[(len(qk_chunks) - 1) % 2], *qk_chunks[-1][1:])

    for c in range(W_QKV // CHUNK):
        t = proj(OFF_V + c * CHUNK)
        for p in range(HP_PER_CHUNK):
            store_head_pair((v0_ref, v1_ref, v2_ref), c * HP_PER_CHUNK + p,
                            t[:, p * HP_WIDTH:(p + 1) * HP_WIDTH])

    for j in range(0, W_ATTN_OUT, CHUNK):
        z = proj(OFF_ZA + j)
        sza_ref[:, j:j + CHUNK] = (z * _sigmoid(z)).astype(bf16)
    for j in range(0, D_MODEL, CHUNK):
        sgc_ref[:, j:j + CHUNK] = _sigmoid(proj(OFF_GC + j)).astype(bf16)
        sga_ref[:, j:j + CHUNK] = _sigmoid(proj(OFF_GA + j)).astype(bf16)


def _in_proj(x, norm_g, w_in, tables, ones_bd, conv_w, conv_b):
    B, S, _ = x.shape
    nt = S // TM_IN
    const = lambda b, t: (0, 0)
    tok = lambda b, t: (b, t, 0)
    sq = pl.Squeezed()
    tok_spec = lambda w: pl.BlockSpec((sq, TM_IN, w), tok)
    tok_shape = lambda w: jax.ShapeDtypeStruct((B, S, w), bf16)
    grp_shapes = tuple(jax.ShapeDtypeStruct((B, N_HP_GROUP, S // d, d * HP_WIDTH), bf16)
                       for d in DILATIONS)
    grp_specs = tuple(pl.BlockSpec((sq, N_HP_GROUP, TM_IN // d, d * HP_WIDTH),
                                   lambda b, t: (b, 0, t, 0)) for d in DILATIONS)
    return pl.pallas_call(
        _in_proj_body,
        grid=(B, nt),
        in_specs=[
            tok_spec(D_MODEL),
            pl.BlockSpec((1, D_MODEL), const),
            pl.BlockSpec((D_MODEL, IN_WIDTH), const, pipeline_mode=pl.Buffered(1)),
        ] + [pl.BlockSpec((TM_IN, HP_WIDTH), lambda b, t: (t, 0)) for _ in range(4)] + [
            pl.BlockSpec((CHUNK, CHUNK), const),
            pl.BlockSpec((CONV_TAPS, W_CONV), const),
            pl.BlockSpec((1, W_CONV), const),
        ],
        out_specs=(tok_spec(W_CONV), pl.BlockSpec((sq, F32_ROWS, W_CONV), tok)) + grp_specs * 3
                  + (tok_spec(W_ATTN_OUT), tok_spec(D_MODEL), tok_spec(D_MODEL)),
        out_shape=(tok_shape(W_CONV), jax.ShapeDtypeStruct((B, nt * F32_ROWS, W_CONV), f32)) + grp_shapes * 3
                  + (tok_shape(W_ATTN_OUT), tok_shape(D_MODEL), tok_shape(D_MODEL)),
        scratch_shapes=[pltpu.VMEM((TM_IN, D_MODEL), bf16),
                        pltpu.VMEM((N_PERM, PERM_ROWS, HP_WIDTH), f32),
                        pltpu.VMEM((2, TM_IN, CHUNK), f32)],
        compiler_params=pltpu.CompilerParams(
            dimension_semantics=("arbitrary", "arbitrary"),
            vmem_limit_bytes=VMEM_LIMIT_IN),
        name="in_proj",
    )(x, norm_g, w_in, *tables, ones_bd, conv_w, conv_b)


def _scores(q, ks):
    lane = lax.broadcasted_iota(jnp.int32, (BQ, HP_WIDTH), 1)
    head0 = lane < HEAD_DIM
    zero = jnp.zeros_like(q)
    q2 = jnp.concatenate([jnp.where(head0, q, zero), jnp.where(head0, zero, q)], axis=0)
    return lax.dot_general(q2, ks, (((1,), (1,)), ((), ())), preferred_element_type=f32)


def _softmax_pv(s, vs, bias):
    lane = lax.broadcasted_iota(jnp.int32, (BQ, HP_WIDTH), 1)
    head0 = lane < HEAD_DIM
    s = s + bias
    m = jnp.max(s, axis=1, keepdims=True)
    p = jnp.exp2(s - m).astype(bf16)
    vs1 = jnp.concatenate([vs, jnp.ones_like(vs)], axis=1)
    pv = jnp.dot(p, vs1, preferred_element_type=f32)
    acc = jnp.where(head0, pv[:BQ, :HP_WIDTH], pv[BQ:, :HP_WIDTH])
    l_sel = jnp.where(head0, pv[:BQ, HP_WIDTH:], pv[BQ:, HP_WIDTH:])
    m_sel = jnp.where(head0, m[:BQ], m[BQ:])
    return acc, m_sel, l_sel


def _band_attn_body(bias_ref, q0_ref, k0_ref, v0_ref, q1_ref, k1_ref, v1_ref,
                    q2_ref, k2_ref, v2_ref, o_ref,
                    o0_s, m0_s, l0_s, o1_s, m1_s, l1_s, o2_s, m2_s, l2_s, s_scr):
    def addr(n, L):
        if isinstance(n, int):
            qs = n * BQ
            st = min(max(qs - BAND_HALF, 0), L - SPAN)
            return qs, st, (qs - st) // BAND_HALF
        qs = pl.multiple_of(n * BQ, BQ)
        st = pl.multiple_of(jnp.clip(n * BQ - BAND_HALF, 0, L - SPAN), BAND_HALF)
        return qs, st, lax.shift_right_logical(qs - st, BAND_HALF.bit_length() - 1)

    def scores_stage(blocks, slot):
        for j, (q_ref, k_ref, _, cols, n, L, _) in enumerate(blocks):
            qs, st, _ = addr(n, L)
            s_scr[slot, j] = _scores(q_ref[pl.ds(qs, BQ), cols], k_ref[pl.ds(st, SPAN), cols])

    def softmax_stage(blocks, slot, store):
        for j, (_, _, v_ref, cols, n, L, r) in enumerate(blocks):
            qs, st, case = addr(n, L)
            acc, m, l = _softmax_pv(s_scr[slot, j], v_ref[pl.ds(st, SPAN), cols], bias_ref[case])
            store(qs, acc, m, l, r)

    def run_group(trips, blocks_of, store, next_blocks):
        assert trips % STEP_ITERS == 0 and STEP_ITERS % 2 == 0

        def iters(i, last):
            for k in range(STEP_ITERS):
                nxt = blocks_of(i + k + 1) if not (last and k == STEP_ITERS - 1) else next_blocks
                if nxt is not None:
                    scores_stage(nxt, (k + 1) % 2)
                softmax_stage(blocks_of(i + k), k % 2, store)

        def step(h, carry):
            iters(STEP_ITERS * h, False)
            return carry

        lax.fori_loop(0, trips // STEP_ITERS - 1, step, 0)
        iters(trips - STEP_ITERS, True)

    d_last = DILATIONS[-1]
    pitch = o_ref.shape[0] // d_last + RES_PAD

    def make_store(o_s, m_s, l_s, d):
        def store(qs, acc, m, l, r):
            if d == 1:
                rows = pl.ds(qs, BQ)
            elif d == d_last:
                rows = pl.ds(pl.multiple_of(r * pitch + qs, RES_PAD), BQ)
            else:
                rows = pl.ds(qs * d + r, BQ, stride=d)
            o_s[rows, :] = acc
            m_s[rows, :] = m
            l_s[rows, :] = l
        return store

    def blocks_of_group(q_ref, k_ref, v_ref, d):
        L = q_ref.shape[0]
        nb = L // BQ
        if d == 1:
            return nb // PIPE_BLOCKS, lambda i: [
                (q_ref, k_ref, v_ref, slice(None), PIPE_BLOCKS * i + j, L, 0) for j in range(PIPE_BLOCKS)]
        if d <= PIPE_BLOCKS:
            return nb, lambda i: [
                (q_ref, k_ref, v_ref, slice(r * HP_WIDTH, (r + 1) * HP_WIDTH), i, L, r) for r in range(d)]
        per_iter = PIPE_BLOCKS // nb

        def blocks(i):
            out = []
            for jr in range(per_iter):
                r = i * per_iter + jr
                cols = pl.ds(pl.multiple_of(r * HP_WIDTH, HP_WIDTH), HP_WIDTH)
                out += [(q_ref, k_ref, v_ref, cols, n, L, r) for n in range(nb)]
            return out
        return d // per_iter, blocks

    groups = []
    for (q_ref, k_ref, v_ref), o_s, m_s, l_s, d in (
            ((q0_ref, k0_ref, v0_ref), o0_s, m0_s, l0_s, DILATIONS[0]),
            ((q1_ref, k1_ref, v1_ref), o1_s, m1_s, l1_s, DILATIONS[1]),
            ((q2_ref, k2_ref, v2_ref), o2_s, m2_s, l2_s, DILATIONS[2])):
        trips, blocks = blocks_of_group(q_ref, k_ref, v_ref, d)
        groups.append((trips, blocks, make_store(o_s, m_s, l_s, d)))
    scores_stage(groups[0][1](0), 0)
    for g, (trips, blocks, store) in enumerate(groups):
        run_group(trips, blocks, store, groups[g + 1][1](0) if g + 1 < len(groups) else None)

    def merge(c, carry):
        rows = pl.ds(pl.multiple_of(c * BQ, BQ), BQ)

        def last_group(ref):
            base = c * (BQ // d_last)
            return jnp.concatenate([ref[pl.ds(base + a, d_last, stride=pitch), :]
                                    for a in range(BQ // d_last)], axis=0)

        m0, m1, m2 = m0_s[rows, :], m1_s[rows, :], last_group(m2_s)
        mx = jnp.maximum(jnp.maximum(m0, m1), m2)
        w0, w1, w2 = jnp.exp2(m0 - mx), jnp.exp2(m1 - mx), jnp.exp2(m2 - mx)
        num = w0 * o0_s[rows, :] + w1 * o1_s[rows, :] + w2 * last_group(o2_s)
        den = w0 * l0_s[rows, :] + w1 * l1_s[rows, :] + w2 * last_group(l2_s)
        o_ref[rows, :] = (num * (1.0 / den)).astype(bf16)
        return carry

    lax.fori_loop(0, o_ref.shape[0] // BQ, merge, 0, unroll=8)


def _band_bias():
    qi = np.arange(2 * BQ)[:, None] % BQ
    kj = np.arange(SPAN)[None, :]
    rel = np.stack([kj - qi - c * BAND_HALF for c in range(N_BIAS)])
    return jnp.asarray(np.where(np.abs(rel) <= BAND_HALF, 0.0, NEG_BIG), dtype=f32)


def _band_attn(qkv):
    B = qkv[0].shape[0]
    S = qkv[0].shape[2]
    sq = pl.Squeezed()
    hp_block = lambda b, h: (b, h, 0, 0)
    operands, specs = [], []
    for g, d in enumerate(DILATIONS):
        for a in range(3):
            operands.append(qkv[a * N_GROUPS + g])
            specs.append(pl.BlockSpec((sq, sq, S // d, d * HP_WIDTH), hp_block))
    return pl.pallas_call(
        _band_attn_body,
        grid=(B, N_HP_GROUP),
        in_specs=[pl.BlockSpec((N_BIAS, 2 * BQ, SPAN), lambda b, h: (0, 0, 0))] + specs,
        out_specs=pl.BlockSpec((sq, sq, S, HP_WIDTH), hp_block),
        out_shape=jax.ShapeDtypeStruct((B, N_HP_GROUP, S, HP_WIDTH), bf16),
        scratch_shapes=[pltpu.VMEM((S, HP_WIDTH), f32) for _ in range(3 * (N_GROUPS - 1))]
                       + [pltpu.VMEM((S + DILATIONS[-1] * RES_PAD, HP_WIDTH), f32) for _ in range(3)]
                       + [pltpu.VMEM((2, PIPE_BLOCKS, 2 * BQ, SPAN), f32)],
        compiler_params=pltpu.CompilerParams(
            dimension_semantics=("arbitrary", "arbitrary"),
            vmem_limit_bytes=VMEM_LIMIT_ATTN),
        name="band_attn",
    )(_band_bias(), *operands)


def _out_proj_body(x_ref, yc_ref, ep_ref, ec_ref, en_ref, o_ref, sza_ref, sgc_ref, sga_ref,
                   cw_ref, wbc_ref, wba_ref, wo_ref, out_ref):
    t = pl.program_id(1)
    nt = pl.num_programs(1)

    def edge_row(ref, sub, k):
        return ref[sub * F32_ROWS + k:sub * F32_ROWS + k + 1, :]

    corr = []
    for i in range(SUB_TILES):
        u_before = (edge_row(ep_ref, SUB_TILES - 1, 1) * jnp.where(t > 0, 1.0, 0.0) if i == 0
                    else edge_row(ec_ref, i - 1, 1))
        u_after = (edge_row(en_ref, 0, 0) * jnp.where(t < nt - 1, 1.0, 0.0) if i == SUB_TILES - 1
                   else edge_row(ec_ref, i + 1, 0))
        corr += [edge_row(ec_ref, i, 2) * cw_ref[0:1, :] * u_before,
                 edge_row(ec_ref, i, 3) * cw_ref[2:3, :] * u_after]
    corr = jnp.concatenate(corr + [jnp.zeros((BF16_ROWS - len(corr), W_CONV), f32)], axis=0)
    y_ext = jnp.concatenate([yc_ref[...], corr.astype(bf16)], axis=0)
    p_ext = jnp.dot(y_ext, wbc_ref[...], preferred_element_type=f32)
    row = lax.broadcasted_iota(jnp.int32, (F32_ROWS, 1), 0)
    extra = p_ext[TM_OUT:TM_OUT + F32_ROWS]
    pieces = []
    for i in range(SUB_TILES):
        lo, hi = i * TM_IN, (i + 1) * TM_IN
        pieces += [p_ext[lo:lo + F32_ROWS] + jnp.where(row == 0, extra[2 * i:2 * i + 1], 0.0),
                   p_ext[lo + F32_ROWS:hi - F32_ROWS],
                   p_ext[hi - F32_ROWS:hi] + jnp.where(row == F32_ROWS - 1, extra[2 * i + 1:2 * i + 2], 0.0)]
    p_c = jnp.concatenate(pieces, axis=0)
    o_cat = jnp.concatenate([o_ref[h] for h in range(N_HP_GROUP)], axis=1)
    p_a = jnp.dot(o_cat * sza_ref[...], wba_ref[...], preferred_element_type=f32)
    m = (sgc_ref[...].astype(f32) * p_c + sga_ref[...].astype(f32) * p_a).astype(bf16)
    out_ref[...] = x_ref[...] + jnp.dot(m, wo_ref[...], preferred_element_type=f32)


def _out_proj(x, yc, edge, o, sza, sgc, sga, conv_w, wbc, wba, wo):
    B, S, _ = x.shape
    nt = S // TM_OUT
    sq = pl.Squeezed()
    const = lambda b, t: (0, 0)
    tok = lambda b, t: (b, t, 0)
    tok_spec = lambda w: pl.BlockSpec((sq, TM_OUT, w), tok)
    edge_spec = lambda f: pl.BlockSpec((sq, SUB_TILES * F32_ROWS, W_CONV), f)
    return pl.pallas_call(
        _out_proj_body,
        grid=(B, nt),
        in_specs=[
            tok_spec(D_MODEL), tok_spec(W_CONV),
            edge_spec(lambda b, t: (b, jnp.maximum(t - 1, 0), 0)), edge_spec(tok),
            edge_spec(lambda b, t: (b, jnp.minimum(t + 1, nt - 1), 0)),
            pl.BlockSpec((sq, N_HP_GROUP, TM_OUT, HP_WIDTH), lambda b, t: (b, 0, t, 0)),
            tok_spec(W_ATTN_OUT), tok_spec(D_MODEL), tok_spec(D_MODEL),
            pl.BlockSpec((CONV_TAPS, W_CONV), const),
            pl.BlockSpec((W_CONV, D_MODEL), const), pl.BlockSpec((W_ATTN_OUT, D_MODEL), const),
            pl.BlockSpec((D_MODEL, D_MODEL), const),
        ],
        out_specs=tok_spec(D_MODEL),
        out_shape=jax.ShapeDtypeStruct((B, S, D_MODEL), f32),
        compiler_params=pltpu.CompilerParams(
            dimension_semantics=("arbitrary", "arbitrary"),
            vmem_limit_bytes=VMEM_LIMIT_OUT),
        name="out_proj",
    )(x, yc, edge, edge, edge, o, sza, sgc, sga, conv_w, wbc, wba, wo)


def _rotary_tables(S, q_gain, k_gain):
    half = HEAD_DIM // 2
    inv_freq = ROPE_THETA ** (-jnp.arange(0, half, dtype=f32) / half)
    ang = jnp.arange(S, dtype=f32)[:, None] * inv_freq[None, :]
    cos, sin = jnp.cos(ang), jnp.sin(ang)
    reps = HP_WIDTH // HEAD_DIM
    cos_t = jnp.tile(jnp.concatenate([cos, cos], axis=1), (1, reps))
    sin_t = jnp.tile(jnp.concatenate([-sin, sin], axis=1), (1, reps))
    tables = []
    for g, scale in ((q_gain, LOG2E / math.sqrt(HEAD_DIM)), (k_gain, 1.0)):
        g = g.astype(f32) * scale
        g_swap = jnp.concatenate([g[half:], g[:half]])
        tables += [cos_t * jnp.tile(g, reps)[None, :], sin_t * jnp.tile(g_swap, reps)[None, :]]
    return tables


def kernel(x, norm_g, w_in, conv_w, conv_b, q_norm_g, k_norm_g, w_branch_conv, w_branch_attn, w_out):
    B, S, D = x.shape
    assert D == D_MODEL and w_in.shape == (D_MODEL, IN_WIDTH)
    assert S % TM_OUT == 0 and TM_OUT % TM_IN == 0 and 2 * SUB_TILES <= F32_ROWS and (S // DILATIONS[-1]) % BQ == 0
    assert (TM_IN // DILATIONS[-1]) % BF16_ROWS == 0 and (S // BQ) % PIPE_BLOCKS == 0
    assert conv_w.shape == (CONV_TAPS, W_CONV)
    head_id = np.arange(CHUNK) // HEAD_DIM
    ones_bd = jnp.asarray(head_id[:, None] == head_id[None, :], dtype=bf16)
    conv_w = conv_w.astype(f32)

    outs = _in_proj(x, norm_g.astype(f32)[None, :], w_in.astype(bf16),
                    _rotary_tables(S, q_norm_g, k_norm_g), ones_bd, conv_w, conv_b.astype(f32)[None, :])
    yc, edge = outs[0], outs[1]
    sza, sgc, sga = outs[11], outs[12], outs[13]
    o = _band_attn(outs[2:11])
    return _out_proj(x, yc, edge, o, sza, sgc, sga, conv_w,
                     w_branch_conv.astype(bf16), w_branch_attn.astype(bf16), w_out.astype(bf16))
```

```python
import math

import jax
import jax.numpy as jnp
import numpy as np
from jax import lax
from jax.experimental import pallas as pl
from jax.experimental.pallas import tpu as pltpu

D_MODEL = 1024
W_CONV = 1024
HEAD_DIM = 64
HEADS_PER_GROUP = 8
DILATIONS = (1, 4, 16)
BAND_HALF = 64
N_GROUPS = 3
W_QKV = N_GROUPS * HEADS_PER_GROUP * HEAD_DIM
W_ATTN_OUT = HEADS_PER_GROUP * HEAD_DIM
ROPE_THETA = 10000.0
NORM_EPS = 1e-6

OFF_B, OFF_C, OFF_H, OFF_ZC = (i * W_CONV for i in range(4))
OFF_Q = 4 * W_CONV
OFF_K = OFF_Q + W_QKV
OFF_V = OFF_K + W_QKV
OFF_ZA = OFF_V + W_QKV
OFF_GC = OFF_ZA + W_ATTN_OUT
OFF_GA = OFF_GC + D_MODEL
IN_WIDTH = OFF_GA + D_MODEL

MXU_WIDTH = 256
HP_WIDTH = 2 * HEAD_DIM
N_HP_GROUP = HEADS_PER_GROUP // 2
CONV_TAPS = 3
EDGE_ROWS = 4
F32_ROWS = 8
BF16_ROWS = 16

TM_IN = 512
TM_OUT = 1024
SUB_TILES = TM_OUT // TM_IN
CHUNK = MXU_WIDTH
HP_PER_CHUNK = CHUNK // HP_WIDTH
PERM_FREE_STRIDE = 4
N_PERM = 4
BQ = 128
SPAN = BQ + 2 * BAND_HALF
RES_PAD = 4
STEP_ITERS = 4
PIPE_BLOCKS = 4
N_BIAS = SPAN // BAND_HALF - 1
NEG_BIG = -1e30
LOG2E = math.log2(math.e)

VMEM_LIMIT_IN = 58 * 1024 * 1024
VMEM_LIMIT_ATTN = 48 * 1024 * 1024
VMEM_LIMIT_OUT = 56 * 1024 * 1024

f32 = jnp.float32
bf16 = jnp.bfloat16


def _perm_pitch(d):
    return d if d <= PERM_FREE_STRIDE else d + PERM_FREE_STRIDE


PERM_ROWS = max(TM_IN // d * _perm_pitch(d) for d in DILATIONS)


def _sigmoid(z):
    return 1.0 / (1.0 + jnp.exp(-z))


def _in_proj_body(x_ref, ng_ref, w_ref, cq_ref, sq_ref, ck_ref, sk_ref, ones_ref, cw_ref, cb_ref,
                  yc_ref, edge_ref, q0_ref, q1_ref, q2_ref, k0_ref, k1_ref, k2_ref,
                  v0_ref, v1_ref, v2_ref, sza_ref, sgc_ref, sga_ref, xn_ref, perm_ref, t_ref):
    x = x_ref[...]
    xn_ref[...] = (x * ng_ref[...]).astype(bf16)
    rn = lax.rsqrt(jnp.mean(x * x, axis=-1, keepdims=True) + NORM_EPS)

    def proj(c0):
        return jnp.dot(xn_ref[...], w_ref[:, c0:c0 + CHUNK], preferred_element_type=f32)

    n_perm_used = [0]

    def store_head_pair(group_refs, hp, val):
        g, h = divmod(hp, N_HP_GROUP)
        d = DILATIONS[g]
        if d == 1:
            group_refs[g][h] = val.astype(bf16)
            return
        buf = perm_ref.at[n_perm_used[0] % N_PERM]
        n_perm_used[0] += 1
        pitch = _perm_pitch(d)
        if pitch == d:
            buf[0:TM_IN, :] = val
        else:
            for j in range(TM_IN // d):
                buf[j * pitch:j * pitch + d, :] = val[j * d:(j + 1) * d, :]
        for r in range(d):
            group_refs[g][h, :, r * HP_WIDTH:(r + 1) * HP_WIDTH] = (
                buf[pl.ds(r, TM_IN // d, stride=pitch), :].astype(bf16))

    row = lax.broadcasted_iota(jnp.int32, (F32_ROWS, 1), 0)
    edge_ref[EDGE_ROWS:, :] = jnp.zeros((F32_ROWS - EDGE_ROWS, W_CONV), f32)
    for j in range(0, W_CONV, CHUNK):
        cols = slice(j, j + CHUNK)
        u = (proj(OFF_C + j) * rn) * (proj(OFF_H + j) * rn)
        z = proj(OFF_ZC + j) * rn
        a = proj(OFF_B + j) * rn * z * _sigmoid(z)
        rolled = pltpu.roll(u, 1, axis=0)
        u_prev = jnp.concatenate([jnp.where(row == 0, 0.0, rolled[:F32_ROWS]), rolled[F32_ROWS:]], axis=0)
        rolled = pltpu.roll(u, TM_IN - 1, axis=0)
        u_next = jnp.concatenate([rolled[:-F32_ROWS],
                                  jnp.where(row == F32_ROWS - 1, 0.0, rolled[-F32_ROWS:])], axis=0)
        conv = (u_prev * cw_ref[0:1, cols] + u * cw_ref[1:2, cols] + u_next * cw_ref[2:3, cols]
                + cb_ref[:, cols])
        yc_ref[:, cols] = (a * conv).astype(bf16)
        edge_ref[0:1, cols] = u[0:1]
        edge_ref[1:2, cols] = u[TM_IN - 1:TM_IN]
        edge_ref[2:3, cols] = a[0:1]
        edge_ref[3:4, cols] = a[TM_IN - 1:TM_IN]

    lane = lax.broadcasted_iota(jnp.int32, (1, HP_WIDTH), 1)
    first_half = (lane % HEAD_DIM) < (HEAD_DIM // 2)

    def qk_epilogue(t, cos_ref, sin_ref, refs, c):
        ss = jnp.dot((t * t).astype(bf16), ones_ref[...], preferred_element_type=f32)
        r = lax.rsqrt(ss * (1.0 / HEAD_DIM) + NORM_EPS)
        for p in range(HP_PER_CHUNK):
            cols = slice(p * HP_WIDTH, (p + 1) * HP_WIDTH)
            th = t[:, cols]
            partner = jnp.where(first_half,
                                pltpu.roll(th, HP_WIDTH - HEAD_DIM // 2, axis=1),
                                pltpu.roll(th, HEAD_DIM // 2, axis=1))
            store_head_pair(refs, c * HP_PER_CHUNK + p,
                            (th * cos_ref[...] + partner * sin_ref[...]) * r[:, cols])

    qk_chunks = [(OFF_Q + c * CHUNK, cq_ref, sq_ref, (q0_ref, q1_ref, q2_ref), c)
                 for c in range(W_QKV // CHUNK)]
    qk_chunks += [(OFF_K + c * CHUNK, ck_ref, sk_ref, (k0_ref, k1_ref, k2_ref), c)
                  for c in range(W_QKV // CHUNK)]
    for i, ch in enumerate(qk_chunks):
        t_ref[i % 2] = proj(ch[0]) * rn
        if i > 0:
            qk_epilogue(t_ref[(i - 1) % 2], *qk_chunks[i - 1][1:])
    qk_epilogue(t_ref[(len(qk_chunks) - 1) % 2], *qk_chunks[-1][1:])

    for c in range(W_QKV // CHUNK):
        t = proj(OFF_V + c * CHUNK) * rn
        for p in range(HP_PER_CHUNK):
            store_head_pair((v0_ref, v1_ref, v2_ref), c * HP_PER_CHUNK + p,
                            t[:, p * HP_WIDTH:(p + 1) * HP_WIDTH])

    for j in range(0, W_ATTN_OUT, CHUNK):
        z = proj(OFF_ZA + j) * rn
        sza_ref[:, j:j + CHUNK] = (z * _sigmoid(z)).astype(bf16)
    for j in range(0, D_MODEL, CHUNK):
        sgc_ref[:, j:j + CHUNK] = _sigmoid(proj(OFF_GC + j) * rn).astype(bf16)
        sga_ref[:, j:j + CHUNK] = _sigmoid(proj(OFF_GA + j) * rn).astype(bf16)


def _in_proj(x, norm_g, w_in, tables, ones_bd, conv_w, conv_b):
    B, S, _ = x.shape
    nt = S // TM_IN
    const = lambda b, t: (0, 0)
    tok = lambda b, t: (b, t, 0)
    sq = pl.Squeezed()
    tok_spec = lambda w: pl.BlockSpec((sq, TM_IN, w), tok)
    tok_shape = lambda w: jax.ShapeDtypeStruct((B, S, w), bf16)
    grp_shapes = tuple(jax.ShapeDtypeStruct((B, N_HP_GROUP, S // d, d * HP_WIDTH), bf16)
                       for d in DILATIONS)
    grp_specs = tuple(pl.BlockSpec((sq, N_HP_GROUP, TM_IN // d, d * HP_WIDTH),
                                   lambda b, t: (b, 0, t, 0)) for d in DILATIONS)
    return pl.pallas_call(
        _in_proj_body,
        grid=(B, nt),
        in_specs=[
            tok_spec(D_MODEL),
            pl.BlockSpec((1, D_MODEL), const),
            pl.BlockSpec((D_MODEL, IN_WIDTH), const, pipeline_mode=pl.Buffered(1)),
        ] + [pl.BlockSpec((TM_IN, HP_WIDTH), lambda b, t: (t, 0)) for _ in range(4)] + [
            pl.BlockSpec((CHUNK, CHUNK), const),
            pl.BlockSpec((CONV_TAPS, W_CONV), const),
            pl.BlockSpec((1, W_CONV), const),
        ],
        out_specs=(tok_spec(W_CONV), pl.BlockSpec((sq, F32_ROWS, W_CONV), tok)) + grp_specs * 3
                  + (tok_spec(W_ATTN_OUT), tok_spec(D_MODEL), tok_spec(D_MODEL)),
        out_shape=(tok_shape(W_CONV), jax.ShapeDtypeStruct((B, nt * F32_ROWS, W_CONV), f32)) + grp_shapes * 3
                  + (tok_shape(W_ATTN_OUT), tok_shape(D_MODEL), tok_shape(D_MODEL)),
        scratch_shapes=[pltpu.VMEM((TM_IN, D_MODEL), bf16),
                        pltpu.VMEM((N_PERM, PERM_ROWS, HP_WIDTH), f32),
                        pltpu.VMEM((2, TM_IN, CHUNK), f32)],
        compiler_params=pltpu.CompilerParams(
            dimension_semantics=("arbitrary", "arbitrary"),
            vmem_limit_bytes=VMEM_LIMIT_IN),
        name="in_proj",
    )(x, norm_g, w_in, *tables, ones_bd, conv_w, conv_b)


def _scores(q, ks):
    lane = lax.broadcasted_iota(jnp.int32, (BQ, HP_WIDTH), 1)
    head0 = lane < HEAD_DIM
    zero = jnp.zeros_like(q)
    q2 = jnp.concatenate([jnp.where(head0, q, zero), jnp.where(head0, zero, q)], axis=0)
    return lax.dot_general(q2, ks, (((1,), (1,)), ((), ())), preferred_element_type=f32)


def _softmax_pv(s, vs, bias):
    lane = lax.broadcasted_iota(jnp.int32, (BQ, HP_WIDTH), 1)
    head0 = lane < HEAD_DIM
    s = s + bias
    m = jnp.max(s, axis=1, keepdims=True)
    p = jnp.exp2(s - m).astype(bf16)
    vs1 = jnp.concatenate([vs, jnp.ones_like(vs)], axis=1)
    pv = jnp.dot(p, vs1, preferred_element_type=f32)
    acc = jnp.where(head0, pv[:BQ, :HP_WIDTH], pv[BQ:, :HP_WIDTH])
    l_sel = jnp.where(head0, pv[:BQ, HP_WIDTH:], pv[BQ:, HP_WIDTH:])
    m_sel = jnp.where(head0, m[:BQ], m[BQ:])
    return acc, m_sel, l_sel


def _band_attn_body(bias_ref, q0_ref, k0_ref, v0_ref, q1_ref, k1_ref, v1_ref,
                    q2_ref, k2_ref, v2_ref, o_ref,
                    o0_s, m0_s, l0_s, o1_s, m1_s, l1_s, o2_s, m2_s, l2_s, s_scr):
    def addr(n, L):
        if isinstance(n, int):
            qs = n * BQ
            st = min(max(qs - BAND_HALF, 0), L - SPAN)
            return qs, st, (qs - st) // BAND_HALF
        qs = pl.multiple_of(n * BQ, BQ)
        st = pl.multiple_of(jnp.clip(n * BQ - BAND_HALF, 0, L - SPAN), BAND_HALF)
        return qs, st, lax.shift_right_logical(qs - st, BAND_HALF.bit_length() - 1)

    def scores_stage(blocks, slot):
        for j, (q_ref, k_ref, _, cols, n, L, _) in enumerate(blocks):
            qs, st, _ = addr(n, L)
            s_scr[slot, j] = _scores(q_ref[pl.ds(qs, BQ), cols], k_ref[pl.ds(st, SPAN), cols])

    def softmax_stage(blocks, slot, store):
        for j, (_, _, v_ref, cols, n, L, r) in enumerate(blocks):
            qs, st, case = addr(n, L)
            acc, m, l = _softmax_pv(s_scr[slot, j], v_ref[pl.ds(st, SPAN), cols], bias_ref[case])
            store(qs, acc, m, l, r)

    def run_group(trips, blocks_of, store, next_blocks):
        assert trips % STEP_ITERS == 0 and STEP_ITERS % 2 == 0

        def iters(i, last):
            for k in range(STEP_ITERS):
                nxt = blocks_of(i + k + 1) if not (last and k == STEP_ITERS - 1) else next_blocks
                if nxt is not None:
                    scores_stage(nxt, (k + 1) % 2)
                softmax_stage(blocks_of(i + k), k % 2, store)

        def step(h, carry):
            iters(STEP_ITERS * h, False)
            return carry

        lax.fori_loop(0, trips // STEP_ITERS - 1, step, 0)
        iters(trips - STEP_ITERS, True)

    d_last = DILATIONS[-1]
    pitch = o_ref.shape[0] // d_last + RES_PAD

    def make_store(o_s, m_s, l_s, d):
        def store(qs, acc, m, l, r):
            if d == 1:
                rows = pl.ds(qs, BQ)
            elif d == d_last:
                rows = pl.ds(pl.multiple_of(r * pitch + qs, RES_PAD), BQ)
            else:
                rows = pl.ds(qs * d + r, BQ, stride=d)
            o_s[rows, :] = acc
            m_s[rows, :] = m
            l_s[rows, :] = l
        return store

    def blocks_of_group(q_ref, k_ref, v_ref, d):
        L = q_ref.shape[0]
        nb = L // BQ
        if d == 1:
            return nb // PIPE_BLOCKS, lambda i: [
                (q_ref, k_ref, v_ref, slice(None), PIPE_BLOCKS * i + j, L, 0) for j in range(PIPE_BLOCKS)]
        if d <= PIPE_BLOCKS:
            return nb, lambda i: [
                (q_ref, k_ref, v_ref, slice(r * HP_WIDTH, (r + 1) * HP_WIDTH), i, L, r) for r in range(d)]
        per_iter = PIPE_BLOCKS // nb

        def blocks(i):
            out = []
            for jr in range(per_iter):
                r = i * per_iter + jr
                cols = pl.ds(pl.multiple_of(r * HP_WIDTH, HP_WIDTH), HP_WIDTH)
                out += [(q_ref, k_ref, v_ref, cols, n, L, r) for n in range(nb)]
            return out
        return d // per_iter, blocks

    groups = []
    for (q_ref, k_ref, v_ref), o_s, m_s, l_s, d in (
            ((q0_ref, k0_ref, v0_ref), o0_s, m0_s, l0_s, DILATIONS[0]),
            ((q1_ref, k1_ref, v1_ref), o1_s, m1_s, l1_s, DILATIONS[1]),
            ((q2_ref, k2_ref, v2_ref), o2_s, m2_s, l2_s, DILATIONS[2])):
        trips, blocks = blocks_of_group(q_ref, k_ref, v_ref, d)
        groups.append((trips, blocks, make_store(o_s, m_s, l_s, d)))
    scores_stage(groups[0][1](0), 0)
    for g, (trips, blocks, store) in enumerate(groups):
        run_group(trips, blocks, store, groups[g + 1][1](0) if g + 1 < len(groups) else None)

    def merge(c, carry):
        rows = pl.ds(pl.multiple_of(c * BQ, BQ), BQ)

        def last_group(ref):
            base = c * (BQ // d_last)
            return jnp.concatenate([ref[pl.ds(base + a, d_last, stride=pitch), :]
                                    for a in range(BQ // d_last)], axis=0)

        m0, m1, m2 = m0_s[rows, :], m1_s[rows, :], last_group(m2_s)
        mx = jnp.maximum(jnp.maximum(m0, m1), m2)
        w0, w1, w2 = jnp.exp2(m0 - mx), jnp.exp2(m1 - mx), jnp.exp2(m2 - mx)
        num = w0 * o0_s[rows, :] + w1 * o1_s[rows, :] + w2 * last_group(o2_s)
        den = w0 * l0_s[rows, :] + w1 * l1_s[rows, :] + w2 * last_group(l2_s)
        o_ref[rows, :] = (num * (1.0 / den)).astype(bf16)
        return carry

    lax.fori_loop(0, o_ref.shape[0] // BQ, merge, 0, unroll=8)


def _band_bias():
    qi = np.arange(2 * BQ)[:, None] % BQ
    kj = np.arange(SPAN)[None, :]
    rel = np.stack([kj - qi - c * BAND_HALF for c in range(N_BIAS)])
    return jnp.asarray(np.where(np.abs(rel) <= BAND_HALF, 0.0, NEG_BIG), dtype=f32)


def _band_attn(qkv):
    B = qkv[0].shape[0]
    S = qkv[0].shape[2]
    sq = pl.Squeezed()
    hp_block = lambda b, h: (b, h, 0, 0)
    operands, specs = [], []
    for g, d in enumerate(DILATIONS):
        for a in range(3):
            operands.append(qkv[a * N_GROUPS + g])
            specs.append(pl.BlockSpec((sq, sq, S // d, d * HP_WIDTH), hp_block))
    return pl.pallas_call(
        _band_attn_body,
        grid=(B, N_HP_GROUP),
        in_specs=[pl.BlockSpec((N_BIAS, 2 * BQ, SPAN), lambda b, h: (0, 0, 0))] + specs,
        out_specs=pl.BlockSpec((sq, sq, S, HP_WIDTH), hp_block),
        out_shape=jax.ShapeDtypeStruct((B, N_HP_GROUP, S, HP_WIDTH), bf16),
        scratch_shapes=[pltpu.VMEM((S, HP_WIDTH), f32) for _ in range(3 * (N_GROUPS - 1))]
                       + [pltpu.VMEM((S + DILATIONS[-1] * RES_PAD, HP_WIDTH), f32) for _ in range(3)]
                       + [pltpu.VMEM((2, PIPE_BLOCKS, 2 * BQ, SPAN), f32)],
        compiler_params=pltpu.CompilerParams(
            dimension_semantics=("arbitrary", "arbitrary"),
            vmem_limit_bytes=VMEM_LIMIT_ATTN),
        name="band_attn",
    )(_band_bias(), *operands)


def _out_proj_body(x_ref, yc_ref, ep_ref, ec_ref, en_ref, o_ref, sza_ref, sgc_ref, sga_ref,
                   cw_ref, wbc_ref, wba_ref, wo_ref, out_ref):
    t = pl.program_id(1)
    nt = pl.num_programs(1)

    def edge_row(ref, sub, k):
        return ref[sub * F32_ROWS + k:sub * F32_ROWS + k + 1, :]

    corr = []
    for i in range(SUB_TILES):
        u_before = (edge_row(ep_ref, SUB_TILES - 1, 1) * jnp.where(t > 0, 1.0, 0.0) if i == 0
                    else edge_row(ec_ref, i - 1, 1))
        u_after = (edge_row(en_ref, 0, 0) * jnp.where(t < nt - 1, 1.0, 0.0) if i == SUB_TILES - 1
                   else edge_row(ec_ref, i + 1, 0))
        corr += [edge_row(ec_ref, i, 2) * cw_ref[0:1, :] * u_before,
                 edge_row(ec_ref, i, 3) * cw_ref[2:3, :] * u_after]
    corr = jnp.concatenate(corr + [jnp.zeros((BF16_ROWS - len(corr), W_CONV), f32)], axis=0)
    y_ext = jnp.concatenate([yc_ref[...], corr.astype(bf16)], axis=0)
    p_ext = jnp.dot(y_ext, wbc_ref[...], preferred_element_type=f32)
    row = lax.broadcasted_iota(jnp.int32, (F32_ROWS, 1), 0)
    extra = p_ext[TM_OUT:TM_OUT + F32_ROWS]
    pieces = []
    for i in range(SUB_TILES):
        lo, hi = i * TM_IN, (i + 1) * TM_IN
        pieces += [p_ext[lo:lo + F32_ROWS] + jnp.where(row == 0, extra[2 * i:2 * i + 1], 0.0),
                   p_ext[lo + F32_ROWS:hi - F32_ROWS],
                   p_ext[hi - F32_ROWS:hi] + jnp.where(row == F32_ROWS - 1, extra[2 * i + 1:2 * i + 2], 0.0)]
    p_c = jnp.concatenate(pieces, axis=0)
    o_cat = jnp.concatenate([o_ref[h] for h in range(N_HP_GROUP)], axis=1)
    p_a = jnp.dot(o_cat * sza_ref[...], wba_ref[...], preferred_element_type=f32)
    m = (sgc_ref[...].astype(f32) * p_c + sga_ref[...].astype(f32) * p_a).astype(bf16)
    out_ref[...] = x_ref[...] + jnp.dot(m, wo_ref[...], preferred_element_type=f32)


def _out_proj(x, yc, edge, o, sza, sgc, sga, conv_w, wbc, wba, wo):
    B, S, _ = x.shape
    nt = S // TM_OUT
    sq = pl.Squeezed()
    const = lambda b, t: (0, 0)
    tok = lambda b, t: (b, t, 0)
    tok_spec = lambda w: pl.BlockSpec((sq, TM_OUT, w), tok)
    edge_spec = lambda f: pl.BlockSpec((sq, SUB_TILES * F32_ROWS, W_CONV), f)
    return pl.pallas_call(
        _out_proj_body,
        grid=(B, nt),
        in_specs=[
            tok_spec(D_MODEL), tok_spec(W_CONV),
            edge_spec(lambda b, t: (b, jnp.maximum(t - 1, 0), 0)), edge_spec(tok),
            edge_spec(lambda b, t: (b, jnp.minimum(t + 1, nt - 1), 0)),
            pl.BlockSpec((sq, N_HP_GROUP, TM_OUT, HP_WIDTH), lambda b, t: (b, 0, t, 0)),
            tok_spec(W_ATTN_OUT), tok_spec(D_MODEL), tok_spec(D_MODEL),
            pl.BlockSpec((CONV_TAPS, W_CONV), const),
            pl.BlockSpec((W_CONV, D_MODEL), const), pl.BlockSpec((W_ATTN_OUT, D_MODEL), const),
            pl.BlockSpec((D_MODEL, D_MODEL), const),
        ],
        out_specs=tok_spec(D_MODEL),
        out_shape=jax.ShapeDtypeStruct((B, S, D_MODEL), f32),
        compiler_params=pltpu.CompilerParams(
            dimension_semantics=("arbitrary", "arbitrary"),
            vmem_limit_bytes=VMEM_LIMIT_OUT),
        name="out_proj",
    )(x, yc, edge, edge, edge, o, sza, sgc, sga, conv_w, wbc, wba, wo)


def _rotary_tables(S, q_gain, k_gain):
    half = HEAD_DIM // 2
    inv_freq = ROPE_THETA ** (-jnp.arange(0, half, dtype=f32) / half)
    ang = jnp.arange(S, dtype=f32)[:, None] * inv_freq[None, :]
    cos, sin = jnp.cos(ang), jnp.sin(ang)
    reps = HP_WIDTH // HEAD_DIM
    cos_t = jnp.tile(jnp.concatenate([cos, cos], axis=1), (1, reps))
    sin_t = jnp.tile(jnp.concatenate([-sin, sin], axis=1), (1, reps))
    tables = []
    for g, scale in ((q_gain, LOG2E / math.sqrt(HEAD_DIM)), (k_gain, 1.0)):
        g = g.astype(f32) * scale
        g_swap = jnp.concatenate([g[half:], g[:half]])
        tables += [cos_t * jnp.tile(g, reps)[None, :], sin_t * jnp.tile(g_swap, reps)[None, :]]
    return tables


def kernel(x, norm_g, w_in, conv_w, conv_b, q_norm_g, k_norm_g, w_branch_conv, w_branch_attn, w_out):
    B, S, D = x.shape
    assert D == D_MODEL and w_in.shape == (D_MODEL, IN_WIDTH)
    assert S % TM_OUT == 0 and TM_OUT % TM_IN == 0 and 2 * SUB_TILES <= F32_ROWS and (S // DILATIONS[-1]) % BQ == 0
    assert (TM_IN // DILATIONS[-1]) % BF16_ROWS == 0 and (S // BQ) % PIPE_BLOCKS == 0
    assert conv_w.shape == (CONV_TAPS, W_CONV)
    head_id = np.arange(CHUNK) // HEAD_DIM
    ones_bd = jnp.asarray(head_id[:, None] == head_id[None, :], dtype=bf16)
    conv_w = conv_w.astype(f32)

    outs = _in_proj(x, norm_g.astype(f32)[None, :], w_in.astype(bf16),
                    _rotary_tables(S, q_norm_g, k_norm_g), ones_bd, conv_w, conv_b.astype(f32)[None, :])
    yc, edge = outs[0], outs[1]
    sza, sgc, sga = outs[11], outs[12], outs[13]
    o = _band_attn(outs[2:11])
    return _out_proj(x, yc, edge, o, sza, sgc, sga, conv_w,
                     w_branch_conv.astype(bf16), w_branch_attn.astype(bf16), w_out.astype(bf16))
```
